```python
import math
import jax, jax.numpy as jnp
from jax import lax
import numpy as np

D_MODEL = 1024
BATCH = 1
SEQ = 16384
DEPTH = 4

A_HEADS = 8
A_HEAD_DIM = 64
A_QK = A_HEADS * 2 * A_HEAD_DIM
A_V = A_HEADS * 2 * A_HEAD_DIM
Q_BLOCK = 128
M_HEADS = 4
M_INNER = D_MODEL
M_HEAD_DIM = M_INNER // M_HEADS
CONV_WIDTH = 4
CHUNK = 64
NUM_BUCKETS = 32
MAX_DISTANCE = 128
D_FF = 4 * D_MODEL
N_IN = 2 * A_QK + A_V + 4 * M_INNER + 2 * M_HEADS + 2 * D_MODEL
EPS = 1e-6

kernel_name = "hybrid_diffattn_mlstm_gated_block"


def rms_norm(x, g):
    xf = x.astype(jnp.float32)
    y = xf * lax.rsqrt(jnp.mean(xf * xf, axis=-1, keepdims=True) + EPS)
    return (y * g.astype(jnp.float32)).astype(x.dtype)


def t5_bucket(dist):
    max_exact = NUM_BUCKETS // 2
    nf = jnp.maximum(dist, 1).astype(jnp.float32)
    large = max_exact + (jnp.log(nf / max_exact) / math.log(MAX_DISTANCE / max_exact)
                         * (NUM_BUCKETS - max_exact)).astype(jnp.int32)
    large = jnp.minimum(large, NUM_BUCKETS - 1)
    return jnp.where(dist < max_exact, dist, large)


def causal_conv(u, w, b):
    s = u.shape[1]
    up = jnp.pad(u, ((0, 0), (CONV_WIDTH - 1, 0), (0, 0)))
    y = b
    for j in range(CONV_WIDTH):
        y = y + w[j] * up[:, j:j + s]
    return y


def diff_attention(q1, q2, k1, k2, v, lam, bias_by_dist):
    b, h, s, dh = q1.shape
    nb = s // Q_BLOCK
    scale = A_HEAD_DIM ** -0.5
    kpos = jnp.arange(s)

    def block(args):
        q1b, q2b, start = args
        qpos = start + jnp.arange(Q_BLOCK)
        dist = qpos[:, None] - kpos[None, :]
        causal = dist >= 0
        bias = jnp.take(bias_by_dist, jnp.clip(dist, 0, s - 1), axis=1)[None]

        def probs(qb, kk):
            logits = jnp.einsum('bhqd,bhkd->bhqk', qb, kk).astype(jnp.float32) * scale + bias
            logits = jnp.where(causal, logits, -jnp.inf)
            return jax.nn.softmax(logits, axis=-1)

        p = probs(q1b, k1) - lam * probs(q2b, k2)
        return jnp.einsum('bhqk,bhkd->bhqd', p.astype(v.dtype), v)

    to_blocks = lambda t: t.reshape(b, h, nb, Q_BLOCK, dh).transpose(2, 0, 1, 3, 4)
    starts = jnp.arange(nb) * Q_BLOCK
    out = lax.map(block, (to_blocks(q1), to_blocks(q2), starts))
    return out.transpose(1, 2, 0, 3, 4).reshape(b, h, s, v.shape[-1])


def mlstm_chunkwise(q, k, v, i_pre, f_pre):
    out_dtype = q.dtype
    b, nh, s, dh = q.shape
    nc = s // CHUNK
    qf = q.astype(jnp.float32)
    kf = k.astype(jnp.float32) * (dh ** -0.5)
    vf = v.astype(jnp.float32)
    ig = i_pre.astype(jnp.float32)
    logf = jax.nn.log_sigmoid(f_pre.astype(jnp.float32))
    chunk4 = lambda t: jnp.moveaxis(t.reshape(b, nh, nc, CHUNK, dh), 2, 0)
    chunk3 = lambda t: jnp.moveaxis(t.reshape(b, nh, nc, CHUNK), 2, 0)
    tril = jnp.tril(jnp.ones((CHUNK, CHUNK), dtype=bool))

    def step(carry, inp):
        cmat, nvec, m = carry
        qc, kc, vc, ic, lfc = inp
        bcum = jnp.cumsum(lfc, axis=-1)
        dmat = bcum[..., :, None] - bcum[..., None, :] + ic[..., None, :]
        dmat = jnp.where(tril, dmat, -jnp.inf)
        inter = bcum + m[..., None]
        m_t = jnp.maximum(jnp.max(dmat, axis=-1), inter)
        scores = jnp.einsum('bhtd,bhsd->bhts', qc, kc) * jnp.exp(dmat - m_t[..., None])
        a_inter = jnp.exp(inter - m_t)
        num = (jnp.einsum('bhts,bhsd->bhtd', scores, vc)
               + a_inter[..., None] * jnp.einsum('bhed,bhtd->bhte', cmat, qc))
        den = jnp.sum(scores, axis=-1) + a_inter * jnp.einsum('bhd,bhtd->bht', nvec, qc)
        h = num / jnp.maximum(jnp.abs(den), jnp.exp(-m_t))[..., None]
        b_last = bcum[..., -1]
        g = b_last[..., None] - bcum + ic
        m_new = jnp.maximum(b_last + m, jnp.max(g, axis=-1))
        wk = jnp.exp(g - m_new[..., None])
        decay = jnp.exp(b_last + m - m_new)
        c_new = decay[..., None, None] * cmat + jnp.einsum('bhs,bhse,bhsd->bhed', wk, vc, kc)
        n_new = decay[..., None] * nvec + jnp.einsum('bhs,bhsd->bhd', wk, kc)
        return (c_new, n_new, m_new), h

    init = (jnp.zeros((b, nh, dh, dh), jnp.float32),
            jnp.zeros((b, nh, dh), jnp.float32),
            jnp.zeros((b, nh), jnp.float32))
    _, hs = lax.scan(step, init, (chunk4(qf), chunk4(kf), chunk4(vf), chunk3(ig), chunk3(logf)))
    return jnp.moveaxis(hs, 0, 2).reshape(b, nh, s, dh).astype(out_dtype)


def setup_inputs(seed: int = 0) -> dict:
    key = jax.random.key(seed)
    ks = jax.random.split(key, 24)
    nrm = lambda k, shape, sc: jax.random.normal(k, shape, jnp.float32) * sc
    gain = lambda k, shape: 1.0 + 0.02 * jax.random.normal(k, shape, jnp.float32)
    return {
        "x": nrm(ks[0], (BATCH, SEQ, D_MODEL), 1.0),
        "c": nrm(ks[1], (BATCH, D_MODEL), 1.0),
        "w_ada": nrm(ks[2], (DEPTH, D_MODEL, 6 * D_MODEL), 0.5 * D_MODEL ** -0.5),
        "b_ada": nrm(ks[3], (DEPTH, 6 * D_MODEL), 0.02),
        "norm_mix_g": gain(ks[4], (DEPTH, D_MODEL)),
        "norm_ffn_g": gain(ks[5], (DEPTH, D_MODEL)),
        "w_in": nrm(ks[6], (DEPTH, D_MODEL, N_IN), D_MODEL ** -0.5),
        "b_igate": nrm(ks[7], (DEPTH, M_HEADS), 0.1),
        "b_fgate": jnp.linspace(3.0, 6.0, M_HEADS, dtype=jnp.float32)[None, :] + nrm(ks[8], (DEPTH, M_HEADS), 0.1),
        "qn_g": gain(ks[9], (DEPTH, A_HEAD_DIM)),
        "kn_g": gain(ks[10], (DEPTH, A_HEAD_DIM)),
        "lam_qk": nrm(ks[11], (DEPTH, 4, A_HEAD_DIM), 0.1),
        "subln_g": gain(ks[12], (DEPTH, 2 * A_HEAD_DIM)),
        "rel_table": nrm(ks[13], (NUM_BUCKETS, A_HEADS), 0.5),
        "conv_w": nrm(ks[14], (DEPTH, CONV_WIDTH, 2 * M_INNER), CONV_WIDTH ** -0.5),
        "conv_b": nrm(ks[15], (DEPTH, 2 * M_INNER), 0.02),
        "mhn_g": gain(ks[16], (DEPTH, M_HEAD_DIM)),
        "w_a": nrm(ks[17], (DEPTH, A_V, D_MODEL), A_V ** -0.5),
        "w_m": nrm(ks[18], (DEPTH, M_INNER, D_MODEL), M_INNER ** -0.5),
        "w_out": nrm(ks[19], (DEPTH, D_MODEL, D_MODEL), D_MODEL ** -0.5),
        "w_ff1": nrm(ks[20], (DEPTH, D_MODEL, D_FF), D_MODEL ** -0.5),
        "w_ff2": nrm(ks[21], (DEPTH, D_FF, D_MODEL), D_FF ** -0.5),
    }


def reference(x, c, w_ada, b_ada, norm_mix_g, norm_ffn_g, w_in, b_igate, b_fgate, qn_g, kn_g,
              lam_qk, subln_g, rel_table, conv_w, conv_b, mhn_g, w_a, w_m, w_out, w_ff1, w_ff2):
    b, s, _ = x.shape
    bias_by_dist = rel_table[t5_bucket(jnp.arange(s))].T.astype(jnp.float32)
    cond = jax.nn.silu(c)
    sizes = [A_QK, A_QK, A_V, M_INNER, M_INNER, M_INNER, M_INNER, M_HEADS, M_HEADS, D_MODEL, D_MODEL]
    points = []
    acc = 0
    for sz in sizes[:-1]:
        acc += sz
        points.append(acc)
    m_heads = lambda t: t.reshape(b, s, M_HEADS, M_HEAD_DIM).transpose(0, 2, 1, 3)

    for l in range(DEPTH):
        lam_init = 0.8 - 0.6 * math.exp(-0.3 * l)
        mod = cond @ w_ada[l] + b_ada[l]
        sh_a, sc_a, gt_a, sh_f, sc_f, gt_f = jnp.split(mod[:, None, :], 6, axis=-1)

        h = rms_norm(x, norm_mix_g[l]) * (1.0 + sc_a) + sh_a
        proj = h @ w_in[l]
        aq, ak, av, mq, mk, mv, mo, mi, mf, ga, gm = jnp.split(proj, points, axis=-1)

        aq = rms_norm(aq.reshape(b, s, A_HEADS, 2, A_HEAD_DIM).transpose(3, 0, 2, 1, 4), qn_g[l])
        ak = rms_norm(ak.reshape(b, s, A_HEADS, 2, A_HEAD_DIM).transpose(3, 0, 2, 1, 4), kn_g[l])
        av = av.reshape(b, s, A_HEADS, 2 * A_HEAD_DIM).transpose(0, 2, 1, 3)
        lq = lam_qk[l].astype(jnp.float32)
        lam = jnp.exp(jnp.sum(lq[0] * lq[1])) - jnp.exp(jnp.sum(lq[2] * lq[3])) + lam_init
        ya = diff_attention(aq[0], aq[1], ak[0], ak[1], av, lam, bias_by_dist)
        ya = rms_norm(ya, subln_g[l]) * (1.0 - lam_init)
        ya = ya.transpose(0, 2, 1, 3).reshape(b, s, A_V)

        qk_m = jax.nn.silu(causal_conv(jnp.concatenate([mq, mk], axis=-1), conv_w[l], conv_b[l]))
        mq, mk = jnp.split(qk_m, 2, axis=-1)
        hm = mlstm_chunkwise(m_heads(mq), m_heads(mk), m_heads(mv),
                             (mi + b_igate[l]).transpose(0, 2, 1), (mf + b_fgate[l]).transpose(0, 2, 1))
        hm = rms_norm(hm, mhn_g[l]).transpose(0, 2, 1, 3).reshape(b, s, M_INNER)
        ym = jax.nn.sigmoid(mo) * hm

        merged = jax.nn.sigmoid(ga) * (ya @ w_a[l]) + jax.nn.sigmoid(gm) * (ym @ w_m[l])
        x = x + gt_a * (merged @ w_out[l])

        h = rms_norm(x, norm_ffn_g[l]) * (1.0 + sc_f) + sh_f
        x = x + gt_f * (jnp.square(jax.nn.relu(h @ w_ff1[l])) @ w_ff2[l])
    return x
```

```python
import functools
import math

import jax
import jax.numpy as jnp
from jax import lax
from jax.experimental import pallas as pl
from jax.experimental.pallas import tpu as pltpu

D_MODEL = 1024
A_HEADS = 8
A_HEAD_DIM = 64
A_WIDTH = A_HEADS * 2 * A_HEAD_DIM
M_HEADS = 4
M_HEAD_DIM = D_MODEL // M_HEADS
CONV_WIDTH = 4
NUM_BUCKETS = 32
MAX_DISTANCE = 128
D_FF = 4 * D_MODEL
EPS = 1e-6

LANES = 128
GATE_PAD = LANES
N_MAIN = 9 * D_MODEL
VMEM_LIMIT = 56 * 1024 * 1024
NEG_BIG = -1e30

ROW_TILE = 1024
ATT_BLOCK = 512
M_CHUNK = 256
MLP_TILE = 512
STATE_W = M_HEAD_DIM + LANES


def _f32dot(a, b):
    return jnp.dot(a, b, preferred_element_type=jnp.float32)


def _modulated_norm(x, g, sc, sh):
    y = x * lax.rsqrt(jnp.mean(x * x, axis=-1, keepdims=True) + EPS)
    return (y * g) * (1.0 + sc) + sh


def _ada_kernel(c_ref, w_ref, b_ref, o_ref):
    c = c_ref[...]
    cond = c * jax.nn.sigmoid(c)
    o_ref[...] = jnp.dot(cond, w_ref[...], preferred_element_type=jnp.float32,
                         precision=lax.Precision.HIGHEST) + b_ref[...]


def _ada_modulation(c, w_ada, b_ada):
    depth = w_ada.shape[0]
    c8 = jnp.broadcast_to(c, (8, D_MODEL))
    nblk = w_ada.shape[2] // D_MODEL
    out = pl.pallas_call(
        _ada_kernel,
        grid=(depth, nblk),
        in_specs=[
            pl.BlockSpec((8, D_MODEL), lambda l, j: (0, 0)),
            pl.BlockSpec((None, D_MODEL, D_MODEL), lambda l, j: (l, 0, j)),
            pl.BlockSpec((None, 1, D_MODEL), lambda l, j: (l, 0, j)),
        ],
        out_specs=pl.BlockSpec((None, 8, D_MODEL), lambda l, j: (l, 0, j)),
        out_shape=jax.ShapeDtypeStruct((depth, 8, w_ada.shape[2]), jnp.float32),
        name="ada_modulation",
    )(c8, w_ada, b_ada[:, None, :])
    return out[:, 0, :]


def _proj_kernel(x_ref, g_ref, sc_ref, sh_ref, w_ref, wg_ref, qkg_ref, bd_ref,
                 o_ref, gate_ref, h_scr):
    j = pl.program_id(1)

    @pl.when(j == 0)
    def _():
        h = _modulated_norm(x_ref[...], g_ref[...], sc_ref[...], sh_ref[...])
        hb = h.astype(jnp.bfloat16)
        h_scr[...] = hb
        gate_ref[...] = _f32dot(hb, wg_ref[...])

    y = _f32dot(h_scr[...], w_ref[...])

    @pl.when(j < 2)
    def _():
        bd = bd_ref[...]
        for t in range(D_MODEL // 256):
            yt = y[:, t * 256:(t + 1) * 256]
            sq = yt * yt
            hi = sq.astype(jnp.bfloat16)
            lo = (sq - hi.astype(jnp.float32)).astype(jnp.bfloat16)
            ss = _f32dot(hi, bd) + _f32dot(lo, bd)
            yn = yt * lax.rsqrt(ss * (1.0 / A_HEAD_DIM) + EPS) * qkg_ref[:, t * 256:(t + 1) * 256]
            o_ref[:, t * 256:(t + 1) * 256] = yn.astype(o_ref.dtype)

    @pl.when(j >= 2)
    def _():
        o_ref[...] = y.astype(o_ref.dtype)


def _in_projection(x, g, sc, sh, w_main, w_gate, qk_gain, bd):
    s = x.shape[0]
    tm = min(ROW_TILE, s)
    nblk = N_MAIN // D_MODEL
    vec = pl.BlockSpec((1, D_MODEL), lambda i, j: (0, 0))
    return pl.pallas_call(
        _proj_kernel,
        grid=(s // tm, nblk),
        in_specs=[
            pl.BlockSpec((tm, D_MODEL), lambda i, j: (i, 0)),
            vec, vec, vec,
            pl.BlockSpec((D_MODEL, D_MODEL), lambda i, j: (0, j)),
            pl.BlockSpec((D_MODEL, GATE_PAD), lambda i, j: (0, 0)),
            pl.BlockSpec((None, 1, D_MODEL), lambda i, j: (jnp.minimum(j, 1), 0, 0)),
            pl.BlockSpec((256, 256), lambda i, j: (0, 0)),
        ],
        out_specs=[
            pl.BlockSpec((tm, D_MODEL), lambda i, j: (i, j)),
            pl.BlockSpec((tm, GATE_PAD), lambda i, j: (i, 0)),
        ],
        out_shape=[
            jax.ShapeDtypeStruct((s, N_MAIN), jnp.bfloat16),
            jax.ShapeDtypeStruct((s, GATE_PAD), jnp.float32),
        ],
        scratch_shapes=[pltpu.VMEM((tm, D_MODEL), jnp.bfloat16)],
        compiler_params=pltpu.CompilerParams(
            dimension_semantics=("arbitrary", "arbitrary"), vmem_limit_bytes=VMEM_LIMIT),
        name="in_projection",
    )(x, g, sc, sh, w_main, w_gate, qk_gain, bd)


def _attn_kernel(lam_init, q_ref, k_ref, v_ref, bias_ref, lamqk_ref, g_ref, o_ref,
                 qzt_scr, acc_scr):
    i = pl.program_id(1)
    bq = q_ref.shape[0]
    bk = bq
    dh = A_HEAD_DIM

    q = q_ref[...].astype(jnp.float32)
    lane = lax.broadcasted_iota(jnp.int32, q.shape, 1)
    qt1 = jnp.where(lane < dh, q, 0.0).T
    qt2 = jnp.where(lane >= dh, q, 0.0).T
    qzt_scr[:, :bq] = qt1.astype(jnp.bfloat16)
    qzt_scr[:, bq:] = qt2.astype(jnp.bfloat16)

    def step(j, m, l, first, bias_idx):
        kb = k_ref[pl.ds(pl.multiple_of(j * bk, bk), bk), :]
        vb = v_ref[pl.ds(pl.multiple_of(j * bk, bk), bk), :]
        st = _f32dot(kb, qzt_scr[...])
        if bias_idx is not None:
            st = st + bias_ref[bias_idx]
        m_new = jnp.maximum(m, jnp.max(st, axis=0, keepdims=True))
        alpha = jnp.exp(m - m_new)
        pt = jnp.exp(st - m_new)
        l_new = alpha * l + jnp.sum(pt, axis=0, keepdims=True)
        pv = lax.dot_general(vb, pt.astype(jnp.bfloat16), (((0,), (0,)), ((), ())),
                             preferred_element_type=jnp.float32)
        if first:
            acc_scr[...] = pv
        else:
            acc_scr[...] = acc_scr[...] * alpha + pv
        return m_new, l_new

    m0 = jnp.full((1, 2 * bq), NEG_BIG, jnp.float32)
    l0 = jnp.zeros((1, 2 * bq), jnp.float32)
    m, l = step(i, m0, l0, True, 0)

    def near(ml):
        return step(i - 1, ml[0], ml[1], False, 1)

    m, l = lax.cond(i > 0, near, lambda ml: ml, (m, l))

    def far(j, ml):
        return step(j, ml[0], ml[1], False, None)

    m, l = lax.fori_loop(0, jnp.maximum(i - 1, 0), far, (m, l))

    lq = lamqk_ref[...]
    e1 = jnp.exp(jnp.sum(lq[0:1] * lq[1:2], axis=1, keepdims=True))
    e2 = jnp.exp(jnp.sum(lq[2:3] * lq[3:4], axis=1, keepdims=True))
    lam = e1 - e2 + lam_init

    o = acc_scr[...] / l
    ot = o[:, :bq] - lam * o[:, bq:]
    ms = jnp.mean(ot * ot, axis=0, keepdims=True)
    on = ot * lax.rsqrt(ms + EPS) * g_ref[...] * (1.0 - lam_init)
    o_ref[...] = on.T.astype(o_ref.dtype)


def _diff_attention(proj, bias_tiles, lam_qk, subln_col, lam_init):
    s = proj.shape[0]
    bq = min(ATT_BLOCK, s)
    nq = s // bq
    kcol = A_WIDTH // LANES
    return pl.pallas_call(
        functools.partial(_attn_kernel, lam_init),
        grid=(A_HEADS, nq),
        in_specs=[
            pl.BlockSpec((bq, LANES), lambda h, i: (i, h)),
            pl.BlockSpec((s, LANES), lambda h, i: (0, kcol + h)),
            pl.BlockSpec((s, LANES), lambda h, i: (0, 2 * kcol + h)),
            pl.BlockSpec((None, 2, bq, 2 * bq), lambda h, i: (h, 0, 0, 0)),
            pl.BlockSpec((4, A_HEAD_DIM), lambda h, i: (0, 0)),
            pl.BlockSpec((LANES, 1), lambda h, i: (0, 0)),
        ],
        out_specs=pl.BlockSpec((bq, LANES), lambda h, i: (i, h)),
        out_shape=jax.ShapeDtypeStruct((s, A_WIDTH), jnp.bfloat16),
        scratch_shapes=[
            pltpu.VMEM((LANES, 2 * bq), jnp.bfloat16),
            pltpu.VMEM((LANES, 2 * bq), jnp.float32),
        ],
        compiler_params=pltpu.CompilerParams(
            dimension_semantics=("arbitrary", "arbitrary"), vmem_limit_bytes=VMEM_LIMIT),
        name="diff_attention",
    )(proj, proj, proj, bias_tiles, lam_qk, subln_col)


def _t5_bucket(dist):
    max_exact = NUM_BUCKETS // 2
    nf = jnp.maximum(dist, 1).astype(jnp.float32)
    large = max_exact + (jnp.log(nf / max_exact) / math.log(MAX_DISTANCE / max_exact)
                         * (NUM_BUCKETS - max_exact)).astype(jnp.int32)
    large = jnp.minimum(large, NUM_BUCKETS - 1)
    return jnp.where(dist < max_exact, dist, large)


def _attention_bias_tiles(rel_table, s):
    bq = min(ATT_BLOCK, s)
    dists = jnp.arange(2 * bq)
    bias = rel_table[_t5_bucket(dists)].T.astype(jnp.float32)
    far = rel_table[_t5_bucket(jnp.array([s - 1]))].T.astype(jnp.float32)
    bias = bias - far
    r = jnp.arange(bq)[:, None]
    c = jnp.arange(bq)[None, :]
    d0 = c - r
    t0 = jnp.where(d0[None] >= 0, bias[:, jnp.clip(d0, 0, 2 * bq - 1)], NEG_BIG)
    t1 = bias[:, bq + d0]
    tiles = jnp.stack([t0, t1], axis=1)
    return jnp.concatenate([tiles, tiles], axis=-1)


def _mlstm_kernel(q_ref, k_ref, v_ref, o_ref, gate_ref, gbias_ref, cwq_ref, cwk_ref, cbq_ref, cbk_ref,
                  hg_ref, y_ref, state_scr, m_scr, pq_scr, pk_scr):
    hd = pl.program_id(0)
    c = pl.program_id(1)
    L = q_ref.shape[0]
    dh = M_HEAD_DIM

    @pl.when(c == 0)
    def _():
        state_scr[...] = jnp.zeros_like(state_scr)
        m_scr[...] = jnp.zeros_like(m_scr)
        pq_scr[...] = jnp.zeros_like(pq_scr)
        pk_scr[...] = jnp.zeros_like(pk_scr)

    row = lax.broadcasted_iota(jnp.int32, (L, dh), 0)

    def conv_silu(u_ref, prev_scr, w_ref, b_ref):
        u = u_ref[...].astype(jnp.float32)
        prev = prev_scr[...]
        w = w_ref[...]
        y = b_ref[...] + w[CONV_WIDTH - 1:CONV_WIDTH] * u
        for back in range(1, CONV_WIDTH):
            shifted = jnp.where(row < back, pltpu.roll(prev, back, 0), pltpu.roll(u, back, 0))
            y = y + w[CONV_WIDTH - 1 - back:CONV_WIDTH - back] * shifted
        prev_scr[...] = u
        return y * jax.nn.sigmoid(y)

    qc = conv_silu(q_ref, pq_scr, cwq_ref, cbq_ref)
    kc = conv_silu(k_ref, pk_scr, cwk_ref, cbk_ref) * (dh ** -0.5)
    qb = qc.astype(jnp.bfloat16)
    kb = kc.astype(jnp.bfloat16)

    gb = gbias_ref[...]
    gates = gate_ref[...] + gb
    sub = lax.broadcasted_iota(jnp.int32, gates.shape, 0)
    ig_row = jnp.sum(jnp.where(sub == hd, gates, 0.0), axis=0, keepdims=True)
    fg_row = jnp.sum(jnp.where(sub == hd + M_HEADS, gates, 0.0), axis=0, keepdims=True)
    lf_row = jax.nn.log_sigmoid(fg_row)

    t_idx = lax.broadcasted_iota(jnp.int32, (L, L), 0)
    s_idx = lax.broadcasted_iota(jnp.int32, (L, L), 1)
    tril = s_idx <= t_idx
    eye = s_idx == t_idx

    def to_col(r):
        return jnp.sum(jnp.where(eye, r, 0.0), axis=1, keepdims=True)

    def to_row(col):
        return jnp.sum(jnp.where(eye, col, 0.0), axis=0, keepdims=True)

    bcum_col = jnp.sum(jnp.where(tril, lf_row, 0.0), axis=1, keepdims=True)
    bcum_row = to_row(bcum_col)
    ig_col = to_col(ig_row)
    b_last = bcum_col[L - 1:L, :]
    m_prev = m_scr[0:1, 0:1]

    dmat = jnp.where(tril, bcum_col - bcum_row + ig_row, NEG_BIG)
    inter = bcum_col + m_prev
    m_t = jnp.maximum(jnp.max(dmat, axis=1, keepdims=True), inter)
    qk = lax.dot_general(qb, kb, (((1,), (1,)), ((), ())), preferred_element_type=jnp.float32)
    scores = qk * jnp.exp(dmat - m_t)
    a_inter = jnp.exp(inter - m_t)

    lane_w = lax.broadcasted_iota(jnp.int32, (L, LANES), 1)
    v_ext = jnp.concatenate(
        [v_ref[...], jnp.where(lane_w == 0, 1.0, 0.0).astype(jnp.bfloat16)], axis=1)
    state = state_scr[...]
    res = _f32dot(scores.astype(jnp.bfloat16), v_ext) + a_inter * _f32dot(qb, state.astype(jnp.bfloat16))
    num = res[:, :dh]
    den = res[:, dh:dh + 1]
    h = num / jnp.maximum(jnp.abs(den), jnp.exp(-m_t))

    g_col = b_last - bcum_col + ig_col
    m_new = jnp.maximum(b_last + m_prev, jnp.max(g_col, axis=0, keepdims=True))
    wk = jnp.exp(g_col - m_new)
    decay = jnp.exp(b_last + m_prev - m_new)
    kw = (kc * wk).astype(jnp.bfloat16)
    upd = lax.dot_general(kw, v_ext, (((0,), (0,)), ((), ())), preferred_element_type=jnp.float32)
    state_scr[...] = decay * state + upd
    m_scr[...] = jnp.broadcast_to(m_new, m_scr.shape)

    hn = h * lax.rsqrt(jnp.mean(h * h, axis=-1, keepdims=True) + EPS) * hg_ref[...]
    y_ref[...] = (jax.nn.sigmoid(o_ref[...].astype(jnp.float32)) * hn).astype(y_ref.dtype)


def _mlstm(proj, gates_t, gate_bias, conv_w, conv_b, mhn_g):
    s = proj.shape[0]
    L = min(M_CHUNK, s)
    nc = s // L
    dh = M_HEAD_DIM
    base = 3 * D_MODEL // dh
    nh = M_HEADS
    return pl.pallas_call(
        _mlstm_kernel,
        grid=(nh, nc),
        in_specs=[
            pl.BlockSpec((L, dh), lambda h, c: (c, base + h)),
            pl.BlockSpec((L, dh), lambda h, c: (c, base + nh + h)),
            pl.BlockSpec((L, dh), lambda h, c: (c, base + 2 * nh + h)),
            pl.BlockSpec((L, dh), lambda h, c: (c, base + 3 * nh + h)),
            pl.BlockSpec((2 * nh, L), lambda h, c: (0, c)),
            pl.BlockSpec((2 * nh, 1), lambda h, c: (0, 0)),
            pl.BlockSpec((CONV_WIDTH, dh), lambda h, c: (0, h)),
            pl.BlockSpec((CONV_WIDTH, dh), lambda h, c: (0, nh + h)),
            pl.BlockSpec((1, dh), lambda h, c: (0, h)),
            pl.BlockSpec((1, dh), lambda h, c: (0, nh + h)),
            pl.BlockSpec((1, dh), lambda h, c: (0, 0)),
        ],
        out_specs=pl.BlockSpec((L, dh), lambda h, c: (c, h)),
        out_shape=jax.ShapeDtypeStruct((s, D_MODEL), jnp.bfloat16),
        scratch_shapes=[
            pltpu.VMEM((dh, STATE_W), jnp.float32),
            pltpu.VMEM((8, LANES), jnp.float32),
            pltpu.VMEM((L, dh), jnp.float32),
            pltpu.VMEM((L, dh), jnp.float32),
        ],
        compiler_params=pltpu.CompilerParams(
            dimension_semantics=("arbitrary", "arbitrary"), vmem_limit_bytes=VMEM_LIMIT),
        name="mlstm",
    )(proj, proj, proj, proj, gates_t, gate_bias, conv_w, conv_w, conv_b, conv_b, mhn_g)


def _merge_kernel(x_ref, ya_ref, ym_ref, ga_ref, gm_ref, wa_ref, wm_ref, wo_ref, gt_ref, o_ref):
    a = _f32dot(ya_ref[...], wa_ref[...])
    b = _f32dot(ym_ref[...], wm_ref[...])
    merged = (jax.nn.sigmoid(ga_ref[...].astype(jnp.float32)) * a
              + jax.nn.sigmoid(gm_ref[...].astype(jnp.float32)) * b)
    o_ref[...] = x_ref[...] + gt_ref[...] * _f32dot(merged.astype(jnp.bfloat16), wo_ref[...])


def _merge(x, ya, ym, proj, w_a, w_m, w_out, gt):
    s = x.shape[0]
    tm = min(MLP_TILE, s)
    row = lambda i: (i, 0)
    const = lambda i: (0, 0)
    wspec = pl.BlockSpec((D_MODEL, D_MODEL), const, pipeline_mode=pl.Buffered(1))
    return pl.pallas_call(
        _merge_kernel,
        grid=(s // tm,),
        in_specs=[
            pl.BlockSpec((tm, D_MODEL), row),
            pl.BlockSpec((tm, D_MODEL), row),
            pl.BlockSpec((tm, D_MODEL), row),
            pl.BlockSpec((tm, D_MODEL), lambda i: (i, 7)),
            pl.BlockSpec((tm, D_MODEL), lambda i: (i, 8)),
            wspec, wspec, wspec,
            pl.BlockSpec((1, D_MODEL), const),
        ],
        out_specs=pl.BlockSpec((tm, D_MODEL), row),
        out_shape=jax.ShapeDtypeStruct((s, D_MODEL), jnp.float32),
        compiler_params=pltpu.CompilerParams(
            dimension_semantics=("arbitrary",), vmem_limit_bytes=VMEM_LIMIT),
        name="merge_out_projection",
    )(x, ya, ym, proj, proj, w_a, w_m, w_out, gt)


def _mlp_kernel(x_ref, g_ref, sc_ref, sh_ref, gt_ref, w1_ref, w2_ref, o_ref):
    x = x_ref[...]
    hb = _modulated_norm(x, g_ref[...], sc_ref[...], sh_ref[...]).astype(jnp.bfloat16)
    acc = None
    for t in range(D_FF // D_MODEL):
        u = jnp.maximum(_f32dot(hb, w1_ref[:, t * D_MODEL:(t + 1) * D_MODEL]), 0.0)
        part = _f32dot((u * u).astype(jnp.bfloat16), w2_ref[t * D_MODEL:(t + 1) * D_MODEL, :])
        acc = part if acc is None else acc + part
    o_ref[...] = x + gt_ref[...] * acc


def _mlp(x, g, sc, sh, gt, w1, w2):
    s = x.shape[0]
    tm = min(MLP_TILE, s)
    row = lambda i: (i, 0)
    const = lambda i: (0, 0)
    vec = pl.BlockSpec((1, D_MODEL), const)
    return pl.pallas_call(
        _mlp_kernel,
        grid=(s // tm,),
        in_specs=[
            pl.BlockSpec((tm, D_MODEL), row),
            vec, vec, vec, vec,
            pl.BlockSpec((D_MODEL, D_FF), const, pipeline_mode=pl.Buffered(1)),
            pl.BlockSpec((D_FF, D_MODEL), const, pipeline_mode=pl.Buffered(1)),
        ],
        out_specs=pl.BlockSpec((tm, D_MODEL), row),
        out_shape=jax.ShapeDtypeStruct((s, D_MODEL), jnp.float32),
        compiler_params=pltpu.CompilerParams(
            dimension_semantics=("arbitrary",), vmem_limit_bytes=VMEM_LIMIT),
        name="relu2_mlp",
    )(x, g, sc, sh, gt, w1, w2)


def kernel(x, c, w_ada, b_ada, norm_mix_g, norm_ffn_g, w_in, b_igate, b_fgate, qn_g, kn_g, lam_qk,
           subln_g, rel_table, conv_w, conv_b, mhn_g, w_a, w_m, w_out, w_ff1, w_ff2):
    b, s, d = x.shape
    assert b == 1 and d == D_MODEL
    depth = w_ada.shape[0]
    bf = jnp.bfloat16
    xs = x[0]

    mod = _ada_modulation(c, w_ada, b_ada)
    bias_tiles = _attention_bias_tiles(rel_table, s)
    blk = A_HEAD_DIM
    bd = (jnp.arange(256)[:, None] // blk == jnp.arange(256)[None, :] // blk).astype(bf)

    gate_lo = 7 * D_MODEL
    gate_hi = gate_lo + 2 * M_HEADS
    for l in range(depth):
        lam_init = 0.8 - 0.6 * math.exp(-0.3 * l)
        sh_a, sc_a, gt_a, sh_f, sc_f, gt_f = [mod[l:l + 1, t * d:(t + 1) * d] for t in range(6)]

        w = w_in[l]
        w_main = jnp.concatenate([w[:, :gate_lo], w[:, gate_hi:]], axis=1).astype(bf)
        w_gate = jnp.pad(w[:, gate_lo:gate_hi], ((0, 0), (0, GATE_PAD - 2 * M_HEADS))).astype(bf)
        reps = A_WIDTH // A_HEAD_DIM
        qk_gain = jnp.stack([jnp.tile(qn_g[l], reps) * (A_HEAD_DIM ** -0.5),
                             jnp.tile(kn_g[l], reps)])[:, None, :]

        proj, gates = _in_projection(xs, norm_mix_g[l][None], sc_a, sh_a, w_main, w_gate, qk_gain, bd)

        ya = _diff_attention(proj, bias_tiles, lam_qk[l], subln_g[l][:, None], lam_init)

        gates_t = gates[:, :2 * M_HEADS].T
        gate_bias = jnp.concatenate([b_igate[l], b_fgate[l]])[:, None]
        ym = _mlstm(proj, gates_t, gate_bias, conv_w[l], conv_b[l][None], mhn_g[l][None])

        xs = _merge(xs, ya, ym, proj, w_a[l].astype(bf), w_m[l].astype(bf), w_out[l].astype(bf), gt_a)
        xs = _mlp(xs, norm_ffn_g[l][None], sc_f, sh_f, gt_f, w_ff1[l].astype(bf), w_ff2[l].astype(bf))
    return xs[None]
```

```python
import functools
import math

import jax
import jax.numpy as jnp
from jax import lax
from jax.experimental import pallas as pl
from jax.experimental.pallas import tpu as pltpu

D_MODEL = 1024
A_HEADS = 8
A_HEAD_DIM = 64
A_WIDTH = A_HEADS * 2 * A_HEAD_DIM
M_HEADS = 4
M_HEAD_DIM = D_MODEL // M_HEADS
CONV_WIDTH = 4
NUM_BUCKETS = 32
MAX_DISTANCE = 128
D_FF = 4 * D_MODEL
EPS = 1e-6

LANES = 128
GATE_PAD = LANES
N_MAIN = 9 * D_MODEL
VMEM_LIMIT = 56 * 1024 * 1024
NEG_BIG = -1e30
LOG2E = math.log2(math.e)

ROW_TILE = 1024
ATT_BLOCK = 512
M_CHUNK = 256
MLP_TILE = 512
STATE_W = M_HEAD_DIM + LANES


def _f32dot(a, b):
    return jnp.dot(a, b, preferred_element_type=jnp.float32)


def _modulated_norm(x, g, sc, sh):
    y = x * lax.rsqrt(jnp.mean(x * x, axis=-1, keepdims=True) + EPS)
    return (y * g) * (1.0 + sc) + sh


def _ada_kernel(c_ref, w_ref, b_ref, o_ref):
    c = c_ref[...]
    cond = c * jax.nn.sigmoid(c)
    o_ref[...] = jnp.dot(cond, w_ref[...], preferred_element_type=jnp.float32,
                         precision=lax.Precision.HIGHEST) + b_ref[...]


def _ada_modulation(c, w_ada, b_ada):
    depth = w_ada.shape[0]
    c8 = jnp.broadcast_to(c, (8, D_MODEL))
    nblk = w_ada.shape[2] // D_MODEL
    out = pl.pallas_call(
        _ada_kernel,
        grid=(depth, nblk),
        in_specs=[
            pl.BlockSpec((8, D_MODEL), lambda l, j: (0, 0)),
            pl.BlockSpec((None, D_MODEL, D_MODEL), lambda l, j: (l, 0, j)),
            pl.BlockSpec((None, 1, D_MODEL), lambda l, j: (l, 0, j)),
        ],
        out_specs=pl.BlockSpec((None, 8, D_MODEL), lambda l, j: (l, 0, j)),
        out_shape=jax.ShapeDtypeStruct((depth, 8, w_ada.shape[2]), jnp.float32),
        name="ada_modulation",
    )(c8, w_ada, b_ada[:, None, :])
    return out[:, 0, :]


def _proj_kernel(x_ref, g_ref, sc_ref, sh_ref, w_ref, wg_ref, qkg_ref, bd_ref,
                 o_ref, gate_ref, h_scr):
    j = pl.program_id(1)

    @pl.when(j == 0)
    def _():
        h = _modulated_norm(x_ref[...], g_ref[...], sc_ref[...], sh_ref[...])
        hb = h.astype(jnp.bfloat16)
        h_scr[...] = hb
        gate_ref[...] = _f32dot(hb, wg_ref[...])

    y = _f32dot(h_scr[...], w_ref[...])

    @pl.when(j < 2)
    def _():
        bd = bd_ref[...]
        for t in range(D_MODEL // 256):
            yt = y[:, t * 256:(t + 1) * 256]
            sq = yt * yt
            hi = sq.astype(jnp.bfloat16)
            lo = (sq - hi.astype(jnp.float32)).astype(jnp.bfloat16)
            ss = _f32dot(hi, bd) + _f32dot(lo, bd)
            yn = yt * lax.rsqrt(ss * (1.0 / A_HEAD_DIM) + EPS) * qkg_ref[:, t * 256:(t + 1) * 256]
            o_ref[:, t * 256:(t + 1) * 256] = yn.astype(o_ref.dtype)

    @pl.when(j >= 2)
    def _():
        o_ref[...] = y.astype(o_ref.dtype)


def _in_projection(x, g, sc, sh, w_main, w_gate, qk_gain, bd):
    s = x.shape[0]
    tm = min(ROW_TILE, s)
    nblk = N_MAIN // D_MODEL
    vec = pl.BlockSpec((1, D_MODEL), lambda i, j: (0, 0))
    return pl.pallas_call(
        _proj_kernel,
        grid=(s // tm, nblk),
        in_specs=[
            pl.BlockSpec((tm, D_MODEL), lambda i, j: (i, 0)),
            vec, vec, vec,
            pl.BlockSpec((D_MODEL, D_MODEL), lambda i, j: (0, j)),
            pl.BlockSpec((D_MODEL, GATE_PAD), lambda i, j: (0, 0)),
            pl.BlockSpec((None, 1, D_MODEL), lambda i, j: (jnp.minimum(j, 1), 0, 0)),
            pl.BlockSpec((256, 256), lambda i, j: (0, 0)),
        ],
        out_specs=[
            pl.BlockSpec((tm, D_MODEL), lambda i, j: (i, j)),
            pl.BlockSpec((tm, GATE_PAD), lambda i, j: (i, 0)),
        ],
        out_shape=[
            jax.ShapeDtypeStruct((s, N_MAIN), jnp.bfloat16),
            jax.ShapeDtypeStruct((s, GATE_PAD), jnp.float32),
        ],
        scratch_shapes=[pltpu.VMEM((tm, D_MODEL), jnp.bfloat16)],
        compiler_params=pltpu.CompilerParams(
            dimension_semantics=("arbitrary", "arbitrary"), vmem_limit_bytes=VMEM_LIMIT),
        name="in_projection",
    )(x, g, sc, sh, w_main, w_gate, qk_gain, bd)


def _attn_kernel(lam_init, q_ref, k_ref, vt_ref, bias_ref, lamqk_ref, g_ref, o_ref,
                 qzt_scr, sa_scr, sb_scr, m_scr, acc_scr):
    i = pl.program_id(1)
    bq = q_ref.shape[0]
    bk = bq
    dh = A_HEAD_DIM
    dv = 2 * A_HEAD_DIM

    q = q_ref[...].astype(jnp.float32)
    lane = lax.broadcasted_iota(jnp.int32, q.shape, 1)
    qt1 = jnp.where(lane < dh, q, 0.0).T
    qt2 = jnp.where(lane >= dh, q, 0.0).T
    qzt_scr[:, :bq] = qt1.astype(jnp.bfloat16)
    qzt_scr[:, bq:] = qt2.astype(jnp.bfloat16)
    m_scr[...] = jnp.full(m_scr.shape, NEG_BIG, jnp.float32)
    acc_scr[...] = jnp.zeros(acc_scr.shape, jnp.float32)

    def logits(j, s_ref):
        kb = k_ref[pl.ds(pl.multiple_of(j * bk, bk), bk), :]
        s_ref[...] = _f32dot(kb, qzt_scr[...])

    def consume(j, s_ref, bias_idx):
        st = s_ref[...]
        if bias_idx is not None:
            st = st + bias_ref[bias_idx]
        m_old = m_scr[...]
        m_new = jnp.maximum(m_old, jnp.max(st, axis=0, keepdims=True))
        pt = jnp.exp2(st - m_new).astype(jnp.bfloat16)
        pv = _f32dot(vt_ref[j], pt)
        acc_scr[...] = acc_scr[...] * jnp.exp2(m_old - m_new) + pv
        m_scr[...] = m_new

    NEAR, DIAG = 1, 0
    n_far = jnp.maximum(i - 1, 0)
    logits(0, sa_scr)

    def far_pair(t, carry):
        j = 2 * t
        logits(j + 1, sb_scr)
        consume(j, sa_scr, None)
        logits(j + 2, sa_scr)
        consume(j + 1, sb_scr, None)
        return carry

    lax.fori_loop(0, n_far // 2, far_pair, 0)

    @pl.when(i == 0)
    def _():
        consume(i, sa_scr, DIAG)

    @pl.when(jnp.logical_and(i > 0, n_far % 2 == 0))
    def _():
        logits(i, sb_scr)
        consume(i - 1, sa_scr, NEAR)
        consume(i, sb_scr, DIAG)

    @pl.when(n_far % 2 == 1)
    def _():
        logits(i - 1, sb_scr)
        consume(i - 2, sa_scr, None)
        logits(i, sa_scr)
        consume(i - 1, sb_scr, NEAR)
        consume(i, sa_scr, DIAG)

    lq = lamqk_ref[...]
    e1 = jnp.exp(jnp.sum(lq[0:1] * lq[1:2], axis=1, keepdims=True))
    e2 = jnp.exp(jnp.sum(lq[2:3] * lq[3:4], axis=1, keepdims=True))
    lam = e1 - e2 + lam_init

    o = acc_scr[:dv, :] / acc_scr[dv:dv + 1, :]
    ot = o[:, :bq] - lam * o[:, bq:]
    ms = jnp.mean(ot * ot, axis=0, keepdims=True)
    on = ot * lax.rsqrt(ms + EPS) * g_ref[...] * (1.0 - lam_init)
    o_ref[...] = on.T.astype(o_ref.dtype)


def _diff_attention(proj, vt_ext, bias_tiles, lam_qk, subln_col, lam_init):
    s = proj.shape[0]
    bq = min(ATT_BLOCK, s)
    nq = s // bq
    kcol = A_WIDTH // LANES
    vrows = vt_ext.shape[2]
    return pl.pallas_call(
        functools.partial(_attn_kernel, lam_init),
        grid=(A_HEADS, nq),
        in_specs=[
            pl.BlockSpec((bq, LANES), lambda h, i: (i, h)),
            pl.BlockSpec((s, LANES), lambda h, i: (0, kcol + h)),
            pl.BlockSpec((None, nq, vrows, bq), lambda h, i: (h, 0, 0, 0)),
            pl.BlockSpec((None, 2, bq, 2 * bq), lambda h, i: (h, 0, 0, 0)),
            pl.BlockSpec((4, A_HEAD_DIM), lambda h, i: (0, 0)),
            pl.BlockSpec((LANES, 1), lambda h, i: (0, 0)),
        ],
        out_specs=pl.BlockSpec((bq, LANES), lambda h, i: (i, h)),
        out_shape=jax.ShapeDtypeStruct((s, A_WIDTH), jnp.bfloat16),
        scratch_shapes=[
            pltpu.VMEM((LANES, 2 * bq), jnp.bfloat16),
            pltpu.VMEM((bq, 2 * bq), jnp.float32),
            pltpu.VMEM((bq, 2 * bq), jnp.float32),
            pltpu.VMEM((1, 2 * bq), jnp.float32),
            pltpu.VMEM((vrows, 2 * bq), jnp.float32),
        ],
        compiler_params=pltpu.CompilerParams(
            dimension_semantics=("arbitrary", "arbitrary"), vmem_limit_bytes=VMEM_LIMIT),
        name="diff_attention",
    )(proj, proj, vt_ext, bias_tiles, lam_qk, subln_col)


def _values_transposed(proj, s):
    bk = min(ATT_BLOCK, s)
    dv = 2 * A_HEAD_DIM
    v = proj[:, 2 * A_WIDTH:3 * A_WIDTH].reshape(s // bk, bk, A_HEADS, dv)
    vt = v.transpose(2, 0, 3, 1)
    extra = jnp.zeros((A_HEADS, s // bk, 16, bk), vt.dtype).at[:, :, 0, :].set(1.0)
    return jnp.concatenate([vt, extra], axis=2)


def _t5_bucket(dist):
    max_exact = NUM_BUCKETS // 2
    nf = jnp.maximum(dist, 1).astype(jnp.float32)
    large = max_exact + (jnp.log(nf / max_exact) / math.log(MAX_DISTANCE / max_exact)
                         * (NUM_BUCKETS - max_exact)).astype(jnp.int32)
    large = jnp.minimum(large, NUM_BUCKETS - 1)
    return jnp.where(dist < max_exact, dist, large)


def _attention_bias_tiles(rel_table, s):
    bq = min(ATT_BLOCK, s)
    dists = jnp.arange(2 * bq)
    bias = rel_table[_t5_bucket(dists)].T.astype(jnp.float32)
    far = rel_table[_t5_bucket(jnp.array([s - 1]))].T.astype(jnp.float32)
    bias = (bias - far) * LOG2E
    period = 2 * bq + 1
    masked = jnp.full((A_HEADS, bq + 1), NEG_BIG, jnp.float32)
    u_diag = jnp.concatenate([bias[:, :bq], masked], axis=1)
    u_near = jnp.concatenate([bias[:, bq:], jnp.zeros((A_HEADS, 1), jnp.float32), bias[:, :bq]], axis=1)
    u = jnp.stack([u_diag, u_near], axis=1)
    flat = jnp.tile(u, (1, 1, bq))[:, :, :bq * 2 * bq]
    tiles = flat.reshape(A_HEADS, 2, bq, 2 * bq)[..., :bq]
    return jnp.concatenate([tiles, tiles], axis=-1)


def _mlstm_kernel(q_ref, k_ref, v_ref, o_ref, gate_ref, gbias_ref, cwq_ref, cwk_ref, cbq_ref, cbk_ref,
                  hg_ref, y_ref, state_scr, m_scr, pq_scr, pk_scr):
    hd = pl.program_id(0)
    c = pl.program_id(1)
    L = q_ref.shape[0]
    dh = M_HEAD_DIM

    @pl.when(c == 0)
    def _():
        state_scr[...] = jnp.zeros_like(state_scr)
        m_scr[...] = jnp.zeros_like(m_scr)
        pq_scr[...] = jnp.zeros_like(pq_scr)
        pk_scr[...] = jnp.zeros_like(pk_scr)

    row = lax.broadcasted_iota(jnp.int32, (L, dh), 0)

    def conv_silu(u_ref, prev_scr, w_ref, b_ref):
        u = u_ref[...].astype(jnp.float32)
        prev = prev_scr[...]
        w = w_ref[...]
        y = b_ref[...] + w[CONV_WIDTH - 1:CONV_WIDTH] * u
        for back in range(1, CONV_WIDTH):
            shifted = jnp.where(row < back, pltpu.roll(prev, back, 0), pltpu.roll(u, back, 0))
            y = y + w[CONV_WIDTH - 1 - back:CONV_WIDTH - back] * shifted
        prev_scr[...] = u
        return y * jax.nn.sigmoid(y)

    qc = conv_silu(q_ref, pq_scr, cwq_ref, cbq_ref)
    kc = conv_silu(k_ref, pk_scr, cwk_ref, cbk_ref) * (dh ** -0.5)
    qb = qc.astype(jnp.bfloat16)
    kb = kc.astype(jnp.bfloat16)

    gb = gbias_ref[...]
    gates = gate_ref[...] + gb
    sub = lax.broadcasted_iota(jnp.int32, gates.shape, 0)
    ig_row = jnp.sum(jnp.where(sub == hd, gates, 0.0), axis=0, keepdims=True)
    fg_row = jnp.sum(jnp.where(sub == hd + M_HEADS, gates, 0.0), axis=0, keepdims=True)
    lf_row = jax.nn.log_sigmoid(fg_row)

    t_idx = lax.broadcasted_iota(jnp.int32, (L, L), 0)
    s_idx = lax.broadcasted_iota(jnp.int32, (L, L), 1)
    tril = s_idx <= t_idx
    eye = s_idx == t_idx

    def to_col(r):
        return jnp.sum(jnp.where(eye, r, 0.0), axis=1, keepdims=True)

    def to_row(col):
        return jnp.sum(jnp.where(eye, col, 0.0), axis=0, keepdims=True)

    bcum_col = jnp.sum(jnp.where(tril, lf_row, 0.0), axis=1, keepdims=True)
    bcum_row = to_row(bcum_col)
    ig_col = to_col(ig_row)
    b_last = bcum_col[L - 1:L, :]
    m_prev = m_scr[0:1, 0:1]

    dmat = jnp.where(tril, bcum_col - bcum_row + ig_row, NEG_BIG)
    inter = bcum_col + m_prev
    m_t = jnp.maximum(jnp.max(dmat, axis=1, keepdims=True), inter)
    qk = lax.dot_general(qb, kb, (((1,), (1,)), ((), ())), preferred_element_type=jnp.float32)
    scores = qk * jnp.exp(dmat - m_t)
    a_inter = jnp.exp(inter - m_t)

    lane_w = lax.broadcasted_iota(jnp.int32, (L, LANES), 1)
    v_ext = jnp.concatenate(
        [v_ref[...], jnp.where(lane_w == 0, 1.0, 0.0).astype(jnp.bfloat16)], axis=1)
    state = state_scr[...]
    res = _f32dot(scores.astype(jnp.bfloat16), v_ext) + a_inter * _f32dot(qb, state.astype(jnp.bfloat16))
    num = res[:, :dh]
    den = res[:, dh:dh + 1]
    h = num / jnp.maximum(jnp.abs(den), jnp.exp(-m_t))

    g_col = b_last - bcum_col + ig_col
    m_new = jnp.maximum(b_last + m_prev, jnp.max(g_col, axis=0, keepdims=True))
    wk = jnp.exp(g_col - m_new)
    decay = jnp.exp(b_last + m_prev - m_new)
    kw = (kc * wk).astype(jnp.bfloat16)
    upd = lax.dot_general(kw, v_ext, (((0,), (0,)), ((), ())), preferred_element_type=jnp.float32)
    state_scr[...] = decay * state + upd
    m_scr[...] = jnp.broadcast_to(m_new, m_scr.shape)

    hn = h * lax.rsqrt(jnp.mean(h * h, axis=-1, keepdims=True) + EPS) * hg_ref[...]
    y_ref[...] = (jax.nn.sigmoid(o_ref[...].astype(jnp.float32)) * hn).astype(y_ref.dtype)


def _mlstm(proj, gates_t, gate_bias, conv_w, conv_b, mhn_g):
    s = proj.shape[0]
    L = min(M_CHUNK, s)
    nc = s // L
    dh = M_HEAD_DIM
    base = 3 * D_MODEL // dh
    nh = M_HEADS
    return pl.pallas_call(
        _mlstm_kernel,
        grid=(nh, nc),
        in_specs=[
            pl.BlockSpec((L, dh), lambda h, c: (c, base + h)),
            pl.BlockSpec((L, dh), lambda h, c: (c, base + nh + h)),
            pl.BlockSpec((L, dh), lambda h, c: (c, base + 2 * nh + h)),
            pl.BlockSpec((L, dh), lambda h, c: (c, base + 3 * nh + h)),
            pl.BlockSpec((2 * nh, L), lambda h, c: (0, c)),
            pl.BlockSpec((2 * nh, 1), lambda h, c: (0, 0)),
            pl.BlockSpec((CONV_WIDTH, dh), lambda h, c: (0, h)),
            pl.BlockSpec((CONV_WIDTH, dh), lambda h, c: (0, nh + h)),
            pl.BlockSpec((1, dh), lambda h, c: (0, h)),
            pl.BlockSpec((1, dh), lambda h, c: (0, nh + h)),
            pl.BlockSpec((1, dh), lambda h, c: (0, 0)),
        ],
        out_specs=pl.BlockSpec((L, dh), lambda h, c: (c, h)),
        out_shape=jax.ShapeDtypeStruct((s, D_MODEL), jnp.bfloat16),
        scratch_shapes=[
            pltpu.VMEM((dh, STATE_W), jnp.float32),
            pltpu.VMEM((8, LANES), jnp.float32),
            pltpu.VMEM((L, dh), jnp.float32),
            pltpu.VMEM((L, dh), jnp.float32),
        ],
        compiler_params=pltpu.CompilerParams(
            dimension_semantics=("arbitrary", "arbitrary"), vmem_limit_bytes=VMEM_LIMIT),
        name="mlstm",
    )(proj, proj, proj, proj, gates_t, gate_bias, conv_w, conv_w, conv_b, conv_b, mhn_g)


def _merge_kernel(x_ref, ya_ref, ym_ref, ga_ref, gm_ref, wa_ref, wm_ref, wo_ref, gt_ref, o_ref):
    a = _f32dot(ya_ref[...], wa_ref[...])
    b = _f32dot(ym_ref[...], wm_ref[...])
    merged = (jax.nn.sigmoid(ga_ref[...].astype(jnp.float32)) * a
              + jax.nn.sigmoid(gm_ref[...].astype(jnp.float32)) * b)
    o_ref[...] = x_ref[...] + gt_ref[...] * _f32dot(merged.astype(jnp.bfloat16), wo_ref[...])


def _merge(x, ya, ym, proj, w_a, w_m, w_out, gt):
    s = x.shape[0]
    tm = min(MLP_TILE, s)
    row = lambda i: (i, 0)
    const = lambda i: (0, 0)
    wspec = pl.BlockSpec((D_MODEL, D_MODEL), const, pipeline_mode=pl.Buffered(1))
    return pl.pallas_call(
        _merge_kernel,
        grid=(s // tm,),
        in_specs=[
            pl.BlockSpec((tm, D_MODEL), row),
            pl.BlockSpec((tm, D_MODEL), row),
            pl.BlockSpec((tm, D_MODEL), row),
            pl.BlockSpec((tm, D_MODEL), lambda i: (i, 7)),
            pl.BlockSpec((tm, D_MODEL), lambda i: (i, 8)),
            wspec, wspec, wspec,
            pl.BlockSpec((1, D_MODEL), const),
        ],
        out_specs=pl.BlockSpec((tm, D_MODEL), row),
        out_shape=jax.ShapeDtypeStruct((s, D_MODEL), jnp.float32),
        compiler_params=pltpu.CompilerParams(
            dimension_semantics=("arbitrary",), vmem_limit_bytes=VMEM_LIMIT),
        name="merge_out_projection",
    )(x, ya, ym, proj, proj, w_a, w_m, w_out, gt)


def _mlp_kernel(x_ref, g_ref, sc_ref, sh_ref, gt_ref, w1_ref, w2_ref, o_ref):
    x = x_ref[...]
    hb = _modulated_norm(x, g_ref[...], sc_ref[...], sh_ref[...]).astype(jnp.bfloat16)
    acc = None
    for t in range(D_FF // D_MODEL):
        u = jnp.maximum(_f32dot(hb, w1_ref[:, t * D_MODEL:(t + 1) * D_MODEL]), 0.0)
        part = _f32dot((u * u).astype(jnp.bfloat16), w2_ref[t * D_MODEL:(t + 1) * D_MODEL, :])
        acc = part if acc is None else acc + part
    o_ref[...] = x + gt_ref[...] * acc


def _mlp(x, g, sc, sh, gt, w1, w2):
    s = x.shape[0]
    tm = min(MLP_TILE, s)
    row = lambda i: (i, 0)
    const = lambda i: (0, 0)
    vec = pl.BlockSpec((1, D_MODEL), const)
    return pl.pallas_call(
        _mlp_kernel,
        grid=(s // tm,),
        in_specs=[
            pl.BlockSpec((tm, D_MODEL), row),
            vec, vec, vec, vec,
            pl.BlockSpec((D_MODEL, D_FF), const, pipeline_mode=pl.Buffered(1)),
            pl.BlockSpec((D_FF, D_MODEL), const, pipeline_mode=pl.Buffered(1)),
        ],
        out_specs=pl.BlockSpec((tm, D_MODEL), row),
        out_shape=jax.ShapeDtypeStruct((s, D_MODEL), jnp.float32),
        compiler_params=pltpu.CompilerParams(
            dimension_semantics=("arbitrary",), vmem_limit_bytes=VMEM_LIMIT),
        name="relu2_mlp",
    )(x, g, sc, sh, gt, w1, w2)


def kernel(x, c, w_ada, b_ada, norm_mix_g, norm_ffn_g, w_in, b_igate, b_fgate, qn_g, kn_g, lam_qk,
           subln_g, rel_table, conv_w, conv_b, mhn_g, w_a, w_m, w_out, w_ff1, w_ff2):
    b, s, d = x.shape
    assert b == 1 and d == D_MODEL
    depth = w_ada.shape[0]
    bf = jnp.bfloat16
    xs = x[0]

    mod = _ada_modulation(c, w_ada, b_ada)
    bias_tiles = _attention_bias_tiles(rel_table, s)
    blk = A_HEAD_DIM
    bd = (jnp.arange(256)[:, None] // blk == jnp.arange(256)[None, :] // blk).astype(bf)

    gate_lo = 7 * D_MODEL
    gate_hi = gate_lo + 2 * M_HEADS
    for l in range(depth):
        lam_init = 0.8 - 0.6 * math.exp(-0.3 * l)
        sh_a, sc_a, gt_a, sh_f, sc_f, gt_f = [mod[l:l + 1, t * d:(t + 1) * d] for t in range(6)]

        w = w_in[l]
        w_main = jnp.concatenate([w[:, :gate_lo], w[:, gate_hi:]], axis=1).astype(bf)
        w_gate = jnp.pad(w[:, gate_lo:gate_hi], ((0, 0), (0, GATE_PAD - 2 * M_HEADS))).astype(bf)
        reps = A_WIDTH // A_HEAD_DIM
        qk_gain = jnp.stack([jnp.tile(qn_g[l], reps) * (A_HEAD_DIM ** -0.5 * LOG2E),
                             jnp.tile(kn_g[l], reps)])[:, None, :]

        proj, gates = _in_projection(xs, norm_mix_g[l][None], sc_a, sh_a, w_main, w_gate, qk_gain, bd)

        ya = _diff_attention(proj, _values_transposed(proj, s), bias_tiles, lam_qk[l],
                             subln_g[l][:, None], lam_init)

        gates_t = gates[:, :2 * M_HEADS].T
        gate_bias = jnp.concatenate([b_igate[l], b_fgate[l]])[:, None]
        ym = _mlstm(proj, gates_t, gate_bias, conv_w[l], conv_b[l][None], mhn_g[l][None])

        xs = _merge(xs, ya, ym, proj, w_a[l].astype(bf), w_m[l].astype(bf), w_out[l].astype(bf), gt_a)
        xs = _mlp(xs, norm_ffn_g[l][None], sc_f, sh_f, gt_f, w_ff1[l].astype(bf), w_ff2[l].astype(bf))
    return xs[None]
```

```python
import functools
import math

import jax
import jax.numpy as jnp
from jax import lax
from jax.experimental import pallas as pl
from jax.experimental.pallas import tpu as pltpu

D_MODEL = 1024
A_HEADS = 8
A_HEAD_DIM = 64
A_WIDTH = A_HEADS * 2 * A_HEAD_DIM
M_HEADS = 4
M_HEAD_DIM = D_MODEL // M_HEADS
CONV_WIDTH = 4
NUM_BUCKETS = 32
MAX_DISTANCE = 128
D_FF = 4 * D_MODEL
EPS = 1e-6

LANES = 128
GATE_PAD = LANES
N_MAIN = 9 * D_MODEL
VMEM_LIMIT = 56 * 1024 * 1024
NEG_BIG = -1e30
LOG2E = math.log2(math.e)

ROW_TILE = 1024
ATT_BLOCK = 512
M_CHUNK = 256
MLP_TILE = 512
STATE_W = M_HEAD_DIM + LANES


def _f32dot(a, b):
    return jnp.dot(a, b, preferred_element_type=jnp.float32)


def _modulated_norm(x, g, sc, sh):
    y = x * lax.rsqrt(jnp.mean(x * x, axis=-1, keepdims=True) + EPS)
    return (y * g) * (1.0 + sc) + sh


def _ada_kernel(c_ref, w_ref, b_ref, o_ref):
    c = c_ref[...]
    cond = c * jax.nn.sigmoid(c)
    o_ref[...] = jnp.dot(cond, w_ref[...], preferred_element_type=jnp.float32,
                         precision=lax.Precision.HIGHEST) + b_ref[...]


def _ada_modulation(c, w_ada, b_ada):
    depth = w_ada.shape[0]
    c8 = jnp.broadcast_to(c, (8, D_MODEL))
    nblk = w_ada.shape[2] // D_MODEL
    out = pl.pallas_call(
        _ada_kernel,
        grid=(depth, nblk),
        in_specs=[
            pl.BlockSpec((8, D_MODEL), lambda l, j: (0, 0)),
            pl.BlockSpec((None, D_MODEL, D_MODEL), lambda l, j: (l, 0, j)),
            pl.BlockSpec((None, 1, D_MODEL), lambda l, j: (l, 0, j)),
        ],
        out_specs=pl.BlockSpec((None, 8, D_MODEL), lambda l, j: (l, 0, j)),
        out_shape=jax.ShapeDtypeStruct((depth, 8, w_ada.shape[2]), jnp.float32),
        name="ada_modulation",
    )(c8, w_ada, b_ada[:, None, :])
    return out[:, 0, :]


def _proj_kernel(x_ref, g_ref, sc_ref, sh_ref, w_ref, wg_ref, qkg_ref, bd_ref,
                 o_ref, gate_ref, h_scr):
    j = pl.program_id(1)

    @pl.when(j == 0)
    def _():
        h = _modulated_norm(x_ref[...], g_ref[...], sc_ref[...], sh_ref[...])
        hb = h.astype(jnp.bfloat16)
        h_scr[...] = hb
        gate_ref[...] = _f32dot(hb, wg_ref[...])

    y = _f32dot(h_scr[...], w_ref[...])

    @pl.when(j < 2)
    def _():
        bd = bd_ref[...]
        for t in range(D_MODEL // 256):
            yt = y[:, t * 256:(t + 1) * 256]
            ss = _f32dot((yt * yt).astype(jnp.bfloat16), bd)
            yn = yt * lax.rsqrt(ss * (1.0 / A_HEAD_DIM) + EPS) * qkg_ref[:, t * 256:(t + 1) * 256]
            o_ref[:, t * 256:(t + 1) * 256] = yn.astype(o_ref.dtype)

    @pl.when(j >= 2)
    def _():
        o_ref[...] = y.astype(o_ref.dtype)


def _in_projection(x, g, sc, sh, w_main, w_gate, qk_gain, bd):
    s = x.shape[0]
    tm = min(ROW_TILE, s)
    nblk = N_MAIN // D_MODEL
    vec = pl.BlockSpec((1, D_MODEL), lambda i, j: (0, 0))
    return pl.pallas_call(
        _proj_kernel,
        grid=(s // tm, nblk),
        in_specs=[
            pl.BlockSpec((tm, D_MODEL), lambda i, j: (i, 0)),
            vec, vec, vec,
            pl.BlockSpec((D_MODEL, D_MODEL), lambda i, j: (0, j)),
            pl.BlockSpec((D_MODEL, GATE_PAD), lambda i, j: (0, 0)),
            pl.BlockSpec((None, 1, D_MODEL), lambda i, j: (jnp.minimum(j, 1), 0, 0)),
            pl.BlockSpec((256, 256), lambda i, j: (0, 0)),
        ],
        out_specs=[
            pl.BlockSpec((tm, D_MODEL), lambda i, j: (i, j)),
            pl.BlockSpec((tm, GATE_PAD), lambda i, j: (i, 0)),
        ],
        out_shape=[
            jax.ShapeDtypeStruct((s, N_MAIN), jnp.bfloat16),
            jax.ShapeDtypeStruct((s, GATE_PAD), jnp.float32),
        ],
        scratch_shapes=[pltpu.VMEM((tm, D_MODEL), jnp.bfloat16)],
        compiler_params=pltpu.CompilerParams(
            dimension_semantics=("arbitrary", "arbitrary"), vmem_limit_bytes=VMEM_LIMIT),
        name="in_projection",
    )(x, g, sc, sh, w_main, w_gate, qk_gain, bd)


def _attn_kernel(lam_init, q_ref, k_ref, vt_ref, bias_ref, lamqk_ref, g_ref, o_ref,
                 qzt_scr, sa_scr, sb_scr, mxa_scr, mxb_scr, m_scr, acc_scr):
    i = pl.program_id(1)
    bq = q_ref.shape[0]
    bk = bq
    dh = A_HEAD_DIM
    dv = 2 * A_HEAD_DIM

    q = q_ref[...].astype(jnp.float32)
    lane = lax.broadcasted_iota(jnp.int32, q.shape, 1)
    qt1 = jnp.where(lane < dh, q, 0.0).T
    qt2 = jnp.where(lane >= dh, q, 0.0).T
    qzt_scr[:, :bq] = qt1.astype(jnp.bfloat16)
    qzt_scr[:, bq:] = qt2.astype(jnp.bfloat16)
    m_scr[...] = jnp.full(m_scr.shape, NEG_BIG, jnp.float32)
    acc_scr[...] = jnp.zeros(acc_scr.shape, jnp.float32)

    def logits(j, s_ref, mx_ref):
        kb = k_ref[pl.ds(pl.multiple_of(j * bk, bk), bk), :]
        st = _f32dot(kb, qzt_scr[...])
        s_ref[...] = st
        mx_ref[...] = jnp.max(st, axis=0, keepdims=True)

    def consume(j, s_ref, mx_ref, bias_idx):
        st = s_ref[...]
        if bias_idx is None:
            mx = mx_ref[...]
        else:
            st = st + bias_ref[bias_idx]
            mx = jnp.max(st, axis=0, keepdims=True)
        m_old = m_scr[...]
        m_new = jnp.maximum(m_old, mx)
        pt = jnp.exp2(st - m_new).astype(jnp.bfloat16)
        pv = _f32dot(vt_ref[j], pt)
        acc_scr[...] = acc_scr[...] * jnp.exp2(m_old - m_new) + pv
        m_scr[...] = m_new

    NEAR, DIAG = 1, 0
    n_far = jnp.maximum(i - 1, 0)
    buf_a = (sa_scr, mxa_scr)
    buf_b = (sb_scr, mxb_scr)
    logits(0, *buf_a)

    def far_pair(j):
        logits(j + 1, *buf_b)
        consume(j, *buf_a, None)
        logits(j + 2, *buf_a)
        consume(j + 1, *buf_b, None)

    def far_quad(t, carry):
        far_pair(4 * t)
        far_pair(4 * t + 2)
        return carry

    lax.fori_loop(0, n_far // 4, far_quad, 0)

    @pl.when(n_far % 4 >= 2)
    def _():
        far_pair(4 * (n_far // 4))

    @pl.when(i == 0)
    def _():
        consume(i, *buf_a, DIAG)

    @pl.when(jnp.logical_and(i > 0, n_far % 2 == 0))
    def _():
        logits(i, *buf_b)
        consume(i - 1, *buf_a, NEAR)
        consume(i, *buf_b, DIAG)

    @pl.when(n_far % 2 == 1)
    def _():
        logits(i - 1, *buf_b)
        consume(i - 2, *buf_a, None)
        logits(i, *buf_a)
        consume(i - 1, *buf_b, NEAR)
        consume(i, *buf_a, DIAG)

    lq = lamqk_ref[...]
    e1 = jnp.exp(jnp.sum(lq[0:1] * lq[1:2], axis=1, keepdims=True))
    e2 = jnp.exp(jnp.sum(lq[2:3] * lq[3:4], axis=1, keepdims=True))
    lam = e1 - e2 + lam_init

    o = acc_scr[:dv, :] / acc_scr[dv:dv + 1, :]
    ot = o[:, :bq] - lam * o[:, bq:]
    ms = jnp.mean(ot * ot, axis=0, keepdims=True)
    on = ot * lax.rsqrt(ms + EPS) * g_ref[...] * (1.0 - lam_init)
    o_ref[...] = on.T.astype(o_ref.dtype)


def _diff_attention(proj, vt_ext, bias_tiles, lam_qk, subln_col, lam_init):
    s = proj.shape[0]
    bq = min(ATT_BLOCK, s)
    nq = s // bq
    kcol = A_WIDTH // LANES
    vrows = vt_ext.shape[2]
    return pl.pallas_call(
        functools.partial(_attn_kernel, lam_init),
        grid=(A_HEADS, nq),
        in_specs=[
            pl.BlockSpec((bq, LANES), lambda h, i: (i, h)),
            pl.BlockSpec((s, LANES), lambda h, i: (0, kcol + h)),
            pl.BlockSpec((None, nq, vrows, bq), lambda h, i: (h, 0, 0, 0)),
            pl.BlockSpec((None, 2, bq, 2 * bq), lambda h, i: (h, 0, 0, 0)),
            pl.BlockSpec((4, A_HEAD_DIM), lambda h, i: (0, 0)),
            pl.BlockSpec((LANES, 1), lambda h, i: (0, 0)),
        ],
        out_specs=pl.BlockSpec((bq, LANES), lambda h, i: (i, h)),
        out_shape=jax.ShapeDtypeStruct((s, A_WIDTH), jnp.bfloat16),
        scratch_shapes=[
            pltpu.VMEM((LANES, 2 * bq), jnp.bfloat16),
            pltpu.VMEM((bq, 2 * bq), jnp.float32),
            pltpu.VMEM((bq, 2 * bq), jnp.float32),
            pltpu.VMEM((1, 2 * bq), jnp.float32),
            pltpu.VMEM((1, 2 * bq), jnp.float32),
            pltpu.VMEM((1, 2 * bq), jnp.float32),
            pltpu.VMEM((vrows, 2 * bq), jnp.float32),
        ],
        compiler_params=pltpu.CompilerParams(
            dimension_semantics=("arbitrary", "arbitrary"), vmem_limit_bytes=VMEM_LIMIT),
        name="diff_attention",
    )(proj, proj, vt_ext, bias_tiles, lam_qk, subln_col)


def _values_transposed(proj, s):
    bk = min(ATT_BLOCK, s)
    dv = 2 * A_HEAD_DIM
    v = proj[:, 2 * A_WIDTH:3 * A_WIDTH].reshape(s // bk, bk, A_HEADS, dv)
    vt = v.transpose(2, 0, 3, 1)
    extra = jnp.zeros((A_HEADS, s // bk, 16, bk), vt.dtype).at[:, :, 0, :].set(1.0)
    return jnp.concatenate([vt, extra], axis=2)


def _t5_bucket(dist):
    max_exact = NUM_BUCKETS // 2
    nf = jnp.maximum(dist, 1).astype(jnp.float32)
    large = max_exact + (jnp.log(nf / max_exact) / math.log(MAX_DISTANCE / max_exact)
                         * (NUM_BUCKETS - max_exact)).astype(jnp.int32)
    large = jnp.minimum(large, NUM_BUCKETS - 1)
    return jnp.where(dist < max_exact, dist, large)


def _attention_bias_tiles(rel_table, s):
    bq = min(ATT_BLOCK, s)
    dists = jnp.arange(2 * bq)
    bias = rel_table[_t5_bucket(dists)].T.astype(jnp.float32)
    far = rel_table[_t5_bucket(jnp.array([s - 1]))].T.astype(jnp.float32)
    bias = (bias - far) * LOG2E
    period = 2 * bq + 1
    masked = jnp.full((A_HEADS, bq + 1), NEG_BIG, jnp.float32)
    u_diag = jnp.concatenate([bias[:, :bq], masked], axis=1)
    u_near = jnp.concatenate([bias[:, bq:], jnp.zeros((A_HEADS, 1), jnp.float32), bias[:, :bq]], axis=1)
    u = jnp.stack([u_diag, u_near], axis=1)
    flat = jnp.tile(u, (1, 1, bq))[:, :, :bq * 2 * bq]
    tiles = flat.reshape(A_HEADS, 2, bq, 2 * bq)[..., :bq]
    return jnp.concatenate([tiles, tiles], axis=-1)


def _mlstm_kernel(q_ref, k_ref, v_ref, o_ref, gate_ref, gbias_ref, cwq_ref, cwk_ref, cbq_ref, cbk_ref,
                  hg_ref, y_ref, state_scr, m_scr, pq_scr, pk_scr):
    hd = pl.program_id(0)
    c = pl.program_id(1)
    L = q_ref.shape[0]
    dh = M_HEAD_DIM

    @pl.when(c == 0)
    def _():
        state_scr[...] = jnp.zeros_like(state_scr)
        m_scr[...] = jnp.zeros_like(m_scr)
        pq_scr[...] = jnp.zeros_like(pq_scr)
        pk_scr[...] = jnp.zeros_like(pk_scr)

    row = lax.broadcasted_iota(jnp.int32, (L, dh), 0)

    def conv_silu(u_ref, prev_scr, w_ref, b_ref):
        u = u_ref[...].astype(jnp.float32)
        prev = prev_scr[...]
        w = w_ref[...]
        y = b_ref[...] + w[CONV_WIDTH - 1:CONV_WIDTH] * u
        for back in range(1, CONV_WIDTH):
            shifted = jnp.where(row < back, pltpu.roll(prev, back, 0), pltpu.roll(u, back, 0))
            y = y + w[CONV_WIDTH - 1 - back:CONV_WIDTH - back] * shifted
        prev_scr[...] = u
        return y * jax.nn.sigmoid(y)

    qc = conv_silu(q_ref, pq_scr, cwq_ref, cbq_ref)
    kc = conv_silu(k_ref, pk_scr, cwk_ref, cbk_ref) * (dh ** -0.5)
    qb = qc.astype(jnp.bfloat16)
    kb = kc.astype(jnp.bfloat16)

    gb = gbias_ref[...]
    gates = gate_ref[...] + gb
    sub = lax.broadcasted_iota(jnp.int32, gates.shape, 0)
    ig_row = jnp.sum(jnp.where(sub == hd, gates, 0.0), axis=0, keepdims=True)
    fg_row = jnp.sum(jnp.where(sub == hd + M_HEADS, gates, 0.0), axis=0, keepdims=True)
    lf_row = jax.nn.log_sigmoid(fg_row)

    t_idx = lax.broadcasted_iota(jnp.int32, (L, L), 0)
    s_idx = lax.broadcasted_iota(jnp.int32, (L, L), 1)
    tril = s_idx <= t_idx
    eye = s_idx == t_idx

    def to_col(r):
        return jnp.sum(jnp.where(eye, r, 0.0), axis=1, keepdims=True)

    def to_row(col):
        return jnp.sum(jnp.where(eye, col, 0.0), axis=0, keepdims=True)

    bcum_col = jnp.sum(jnp.where(tril, lf_row, 0.0), axis=1, keepdims=True)
    bcum_row = to_row(bcum_col)
    ig_col = to_col(ig_row)
    b_last = bcum_col[L - 1:L, :]
    m_prev = m_scr[0:1, 0:1]

    dmat = jnp.where(tril, bcum_col - bcum_row + ig_row, NEG_BIG)
    inter = bcum_col + m_prev
    m_t = jnp.maximum(jnp.max(dmat, axis=1, keepdims=True), inter)
    qk = lax.dot_general(qb, kb, (((1,), (1,)), ((), ())), preferred_element_type=jnp.float32)
    scores = qk * jnp.exp(dmat - m_t)
    a_inter = jnp.exp(inter - m_t)

    lane_w = lax.broadcasted_iota(jnp.int32, (L, LANES), 1)
    v_ext = jnp.concatenate(
        [v_ref[...], jnp.where(lane_w == 0, 1.0, 0.0).astype(jnp.bfloat16)], axis=1)
    state = state_scr[...]
    res = _f32dot(scores.astype(jnp.bfloat16), v_ext) + a_inter * _f32dot(qb, state.astype(jnp.bfloat16))
    num = res[:, :dh]
    den = res[:, dh:dh + 1]
    h = num / jnp.maximum(jnp.abs(den), jnp.exp(-m_t))

    g_col = b_last - bcum_col + ig_col
    m_new = jnp.maximum(b_last + m_prev, jnp.max(g_col, axis=0, keepdims=True))
    wk = jnp.exp(g_col - m_new)
    decay = jnp.exp(b_last + m_prev - m_new)
    kw = (kc * wk).astype(jnp.bfloat16)
    upd = lax.dot_general(kw, v_ext, (((0,), (0,)), ((), ())), preferred_element_type=jnp.float32)
    state_scr[...] = decay * state + upd
    m_scr[...] = jnp.broadcast_to(m_new, m_scr.shape)

    hn = h * lax.rsqrt(jnp.mean(h * h, axis=-1, keepdims=True) + EPS) * hg_ref[...]
    y_ref[...] = (jax.nn.sigmoid(o_ref[...].astype(jnp.float32)) * hn).astype(y_ref.dtype)


def _mlstm(proj, gates_t, gate_bias, conv_w, conv_b, mhn_g):
    s = proj.shape[0]
    L = min(M_CHUNK, s)
    nc = s // L
    dh = M_HEAD_DIM
    base = 3 * D_MODEL // dh
    nh = M_HEADS
    return pl.pallas_call(
        _mlstm_kernel,
        grid=(nh, nc),
        in_specs=[
            pl.BlockSpec((L, dh), lambda h, c: (c, base + h)),
            pl.BlockSpec((L, dh), lambda h, c: (c, base + nh + h)),
            pl.BlockSpec((L, dh), lambda h, c: (c, base + 2 * nh + h)),
            pl.BlockSpec((L, dh), lambda h, c: (c, base + 3 * nh + h)),
            pl.BlockSpec((2 * nh, L), lambda h, c: (0, c)),
            pl.BlockSpec((2 * nh, 1), lambda h, c: (0, 0)),
            pl.BlockSpec((CONV_WIDTH, dh), lambda h, c: (0, h)),
            pl.BlockSpec((CONV_WIDTH, dh), lambda h, c: (0, nh + h)),
            pl.BlockSpec((1, dh), lambda h, c: (0, h)),
            pl.BlockSpec((1, dh), lambda h, c: (0, nh + h)),
            pl.BlockSpec((1, dh), lambda h, c: (0, 0)),
        ],
        out_specs=pl.BlockSpec((L, dh), lambda h, c: (c, h)),
        out_shape=jax.ShapeDtypeStruct((s, D_MODEL), jnp.bfloat16),
        scratch_shapes=[
            pltpu.VMEM((dh, STATE_W), jnp.float32),
            pltpu.VMEM((8, LANES), jnp.float32),
            pltpu.VMEM((L, dh), jnp.float32),
            pltpu.VMEM((L, dh), jnp.float32),
        ],
        compiler_params=pltpu.CompilerParams(
            dimension_semantics=("arbitrary", "arbitrary"), vmem_limit_bytes=VMEM_LIMIT),
        name="mlstm",
    )(proj, proj, proj, proj, gates_t, gate_bias, conv_w, conv_w, conv_b, conv_b, mhn_g)


def _merge_kernel(x_ref, ya_ref, ym_ref, ga_ref, gm_ref, wa_ref, wm_ref, wo_ref, gt_ref, o_ref):
    a = _f32dot(ya_ref[...], wa_ref[...])
    b = _f32dot(ym_ref[...], wm_ref[...])
    merged = (jax.nn.sigmoid(ga_ref[...].astype(jnp.float32)) * a
              + jax.nn.sigmoid(gm_ref[...].astype(jnp.float32)) * b)
    o_ref[...] = x_ref[...] + gt_ref[...] * _f32dot(merged.astype(jnp.bfloat16), wo_ref[...])


def _merge(x, ya, ym, proj, w_a, w_m, w_out, gt):
    s = x.shape[0]
    tm = min(MLP_TILE, s)
    row = lambda i: (i, 0)
    const = lambda i: (0, 0)
    wspec = pl.BlockSpec((D_MODEL, D_MODEL), const, pipeline_mode=pl.Buffered(1))
    return pl.pallas_call(
        _merge_kernel,
        grid=(s // tm,),
        in_specs=[
            pl.BlockSpec((tm, D_MODEL), row),
            pl.BlockSpec((tm, D_MODEL), row),
            pl.BlockSpec((tm, D_MODEL), row),
            pl.BlockSpec((tm, D_MODEL), lambda i: (i, 7)),
            pl.BlockSpec((tm, D_MODEL), lambda i: (i, 8)),
            wspec, wspec, wspec,
            pl.BlockSpec((1, D_MODEL), const),
        ],
        out_specs=pl.BlockSpec((tm, D_MODEL), row),
        out_shape=jax.ShapeDtypeStruct((s, D_MODEL), jnp.float32),
        compiler_params=pltpu.CompilerParams(
            dimension_semantics=("arbitrary",), vmem_limit_bytes=VMEM_LIMIT),
        name="merge_out_projection",
    )(x, ya, ym, proj, proj, w_a, w_m, w_out, gt)


def _mlp_kernel(x_ref, g_ref, sc_ref, sh_ref, gt_ref, w1_ref, w2_ref, o_ref):
    x = x_ref[...]
    hb = _modulated_norm(x, g_ref[...], sc_ref[...], sh_ref[...]).astype(jnp.bfloat16)
    acc = None
    for t in range(D_FF // D_MODEL):
        u = jnp.maximum(_f32dot(hb, w1_ref[:, t * D_MODEL:(t + 1) * D_MODEL]), 0.0)
        part = _f32dot((u * u).astype(jnp.bfloat16), w2_ref[t * D_MODEL:(t + 1) * D_MODEL, :])
        acc = part if acc is None else acc + part
    o_ref[...] = x + gt_ref[...] * acc


def _mlp(x, g, sc, sh, gt, w1, w2):
    s = x.shape[0]
    tm = min(MLP_TILE, s)
    row = lambda i: (i, 0)
    const = lambda i: (0, 0)
    vec = pl.BlockSpec((1, D_MODEL), const)
    return pl.pallas_call(
        _mlp_kernel,
        grid=(s // tm,),
        in_specs=[
            pl.BlockSpec((tm, D_MODEL), row),
            vec, vec, vec, vec,
            pl.BlockSpec((D_MODEL, D_FF), const, pipeline_mode=pl.Buffered(1)),
            pl.BlockSpec((D_FF, D_MODEL), const, pipeline_mode=pl.Buffered(1)),
        ],
        out_specs=pl.BlockSpec((tm, D_MODEL), row),
        out_shape=jax.ShapeDtypeStruct((s, D_MODEL), jnp.float32),
        compiler_params=pltpu.CompilerParams(
            dimension_semantics=("arbitrary",), vmem_limit_bytes=VMEM_LIMIT),
        name="relu2_mlp",
    )(x, g, sc, sh, gt, w1, w2)


def kernel(x, c, w_ada, b_ada, norm_mix_g, norm_ffn_g, w_in, b_igate, b_fgate, qn_g, kn_g, lam_qk,
           subln_g, rel_table, conv_w, conv_b, mhn_g, w_a, w_m, w_out, w_ff1, w_ff2):
    b, s, d = x.shape
    assert b == 1 and d == D_MODEL
    depth = w_ada.shape[0]
    bf = jnp.bfloat16
    xs = x[0]

    mod = _ada_modulation(c, w_ada, b_ada)
    bias_tiles = _attention_bias_tiles(rel_table, s)
    blk = A_HEAD_DIM
    bd = (jnp.arange(256)[:, None] // blk == jnp.arange(256)[None, :] // blk).astype(bf)

    gate_lo = 7 * D_MODEL
    gate_hi = gate_lo + 2 * M_HEADS
    for l in range(depth):
        lam_init = 0.8 - 0.6 * math.exp(-0.3 * l)
        sh_a, sc_a, gt_a, sh_f, sc_f, gt_f = [mod[l:l + 1, t * d:(t + 1) * d] for t in range(6)]

        w = w_in[l]
        w_main = jnp.concatenate([w[:, :gate_lo], w[:, gate_hi:]], axis=1).astype(bf)
        w_gate = jnp.pad(w[:, gate_lo:gate_hi], ((0, 0), (0, GATE_PAD - 2 * M_HEADS))).astype(bf)
        reps = A_WIDTH // A_HEAD_DIM
        qk_gain = jnp.stack([jnp.tile(qn_g[l], reps) * (A_HEAD_DIM ** -0.5 * LOG2E),
                             jnp.tile(kn_g[l], reps)])[:, None, :]

        proj, gates = _in_projection(xs, norm_mix_g[l][None], sc_a, sh_a, w_main, w_gate, qk_gain, bd)

        ya = _diff_attention(proj, _values_transposed(proj, s), bias_tiles, lam_qk[l],
                             subln_g[l][:, None], lam_init)

        gates_t = gates[:, :2 * M_HEADS].T
        gate_bias = jnp.concatenate([b_igate[l], b_fgate[l]])[:, None]
        ym = _mlstm(proj, gates_t, gate_bias, conv_w[l], conv_b[l][None], mhn_g[l][None])

        xs = _merge(xs, ya, ym, proj, w_a[l].astype(bf), w_m[l].astype(bf), w_out[l].astype(bf), gt_a)
        xs = _mlp(xs, norm_ffn_g[l][None], sc_f, sh_f, gt_f, w_ff1[l].astype(bf), w_ff2[l].astype(bf))
    return xs[None]
```

```python
import functools
import math

import jax
import jax.numpy as jnp
from jax import lax
from jax.experimental import pallas as pl
from jax.experimental.pallas import tpu as pltpu

D_MODEL = 1024
A_HEADS = 8
A_HEAD_DIM = 64
A_WIDTH = A_HEADS * 2 * A_HEAD_DIM
M_HEADS = 4
M_HEAD_DIM = D_MODEL // M_HEADS
CONV_WIDTH = 4
NUM_BUCKETS = 32
MAX_DISTANCE = 128
D_FF = 4 * D_MODEL
EPS = 1e-6

LANES = 128
GATE_PAD = LANES
N_MAIN = 9 * D_MODEL
VMEM_LIMIT = 56 * 1024 * 1024
NEG_BIG = -1e30
LOG2E = math.log2(math.e)

PROJ_TILE = 256
ATT_BLOCK = 512
M_CHUNK = 256
M_HEADS_PER_STEP = 2
MLP_TILE = 512
STATE_W = M_HEAD_DIM + LANES


def _f32dot(a, b):
    return jnp.dot(a, b, preferred_element_type=jnp.float32)


def _modulated_norm(x, g, sc, sh):
    y = x * lax.rsqrt(jnp.mean(x * x, axis=-1, keepdims=True) + EPS)
    return (y * g) * (1.0 + sc) + sh


def _ada_kernel(c_ref, w_ref, b_ref, o_ref):
    c = c_ref[...]
    cond = c * jax.nn.sigmoid(c)
    o_ref[...] = jnp.dot(cond, w_ref[...], preferred_element_type=jnp.float32,
                         precision=lax.Precision.HIGHEST) + b_ref[...]


def _ada_modulation(c, w_ada, b_ada):
    depth = w_ada.shape[0]
    c8 = jnp.broadcast_to(c, (8, D_MODEL))
    nblk = w_ada.shape[2] // D_MODEL
    out = pl.pallas_call(
        _ada_kernel,
        grid=(depth, nblk),
        in_specs=[
            pl.BlockSpec((8, D_MODEL), lambda l, j: (0, 0)),
            pl.BlockSpec((None, D_MODEL, D_MODEL), lambda l, j: (l, 0, j)),
            pl.BlockSpec((None, 1, D_MODEL), lambda l, j: (l, 0, j)),
        ],
        out_specs=pl.BlockSpec((None, 8, D_MODEL), lambda l, j: (l, 0, j)),
        out_shape=jax.ShapeDtypeStruct((depth, 8, w_ada.shape[2]), jnp.float32),
        name="ada_modulation",
    )(c8, w_ada, b_ada[:, None, :])
    return out[:, 0, :]


def _proj_kernel(x_ref, g_ref, sc_ref, sh_ref, w_ref, wg_ref, qkg_ref, bd_ref, o_ref, gate_ref):
    h = _modulated_norm(x_ref[...], g_ref[...], sc_ref[...], sh_ref[...])
    hb = h.astype(jnp.bfloat16)
    gate_ref[...] = _f32dot(hb, wg_ref[...])
    bd = bd_ref[...]
    for j in range(N_MAIN // D_MODEL):
        y = _f32dot(hb, w_ref[:, j * D_MODEL:(j + 1) * D_MODEL])
        if j >= 2:
            o_ref[:, j * D_MODEL:(j + 1) * D_MODEL] = y.astype(o_ref.dtype)
            continue
        for t in range(D_MODEL // 256):
            yt = y[:, t * 256:(t + 1) * 256]
            ss = _f32dot((yt * yt).astype(jnp.bfloat16), bd)
            yn = yt * lax.rsqrt(ss * (1.0 / A_HEAD_DIM) + EPS) * qkg_ref[j, :, t * 256:(t + 1) * 256]
            o_ref[:, j * D_MODEL + t * 256:j * D_MODEL + (t + 1) * 256] = yn.astype(o_ref.dtype)


def _in_projection(x, g, sc, sh, w_main, w_gate, qk_gain, bd):
    s = x.shape[0]
    tm = min(PROJ_TILE, s)
    const = lambda i: (0, 0)
    vec = pl.BlockSpec((1, D_MODEL), const)
    resident = pl.Buffered(1)
    return pl.pallas_call(
        _proj_kernel,
        grid=(s // tm,),
        in_specs=[
            pl.BlockSpec((tm, D_MODEL), lambda i: (i, 0)),
            vec, vec, vec,
            pl.BlockSpec((D_MODEL, N_MAIN), const, pipeline_mode=resident),
            pl.BlockSpec((D_MODEL, GATE_PAD), const, pipeline_mode=resident),
            pl.BlockSpec((2, 1, D_MODEL), lambda i: (0, 0, 0)),
            pl.BlockSpec((256, 256), const),
        ],
        out_specs=[
            pl.BlockSpec((tm, N_MAIN), lambda i: (i, 0)),
            pl.BlockSpec((tm, GATE_PAD), lambda i: (i, 0)),
        ],
        out_shape=[
            jax.ShapeDtypeStruct((s, N_MAIN), jnp.bfloat16),
            jax.ShapeDtypeStruct((s, GATE_PAD), jnp.float32),
        ],
        compiler_params=pltpu.CompilerParams(
            dimension_semantics=("arbitrary",), vmem_limit_bytes=VMEM_LIMIT),
        name="in_projection",
    )(x, g, sc, sh, w_main, w_gate, qk_gain, bd)


def _attn_kernel(lam_init, q_ref, k_ref, vt_ref, bias_ref, lamqk_ref, g_ref, o_ref,
                 qzt_scr, sa_scr, sb_scr, mxa_scr, mxb_scr, m_scr, acc_scr):
    i = pl.program_id(1)
    bq = q_ref.shape[0]
    bk = bq
    dh = A_HEAD_DIM
    dv = 2 * A_HEAD_DIM

    q = q_ref[...].astype(jnp.float32)
    lane = lax.broadcasted_iota(jnp.int32, q.shape, 1)
    qt1 = jnp.where(lane < dh, q, 0.0).T
    qt2 = jnp.where(lane >= dh, q, 0.0).T
    qzt_scr[:, :bq] = qt1.astype(jnp.bfloat16)
    qzt_scr[:, bq:] = qt2.astype(jnp.bfloat16)
    m_scr[...] = jnp.full(m_scr.shape, NEG_BIG, jnp.float32)
    acc_scr[...] = jnp.zeros(acc_scr.shape, jnp.float32)

    def logits(j, s_ref, mx_ref):
        kb = k_ref[pl.ds(pl.multiple_of(j * bk, bk), bk), :]
        st = _f32dot(kb, qzt_scr[...])
        s_ref[...] = st
        mx_ref[...] = jnp.max(st, axis=0, keepdims=True)

    def consume(j, s_ref, mx_ref, bias_idx):
        st = s_ref[...]
        if bias_idx is None:
            mx = mx_ref[...]
        else:
            st = st + bias_ref[bias_idx]
            mx = jnp.max(st, axis=0, keepdims=True)
        m_old = m_scr[...]
        m_new = jnp.maximum(m_old, mx)
        pt = jnp.exp2(st - m_new).astype(jnp.bfloat16)
        pv = _f32dot(vt_ref[j], pt)
        acc_scr[...] = acc_scr[...] * jnp.exp2(m_old - m_new) + pv
        m_scr[...] = m_new

    NEAR, DIAG = 1, 0
    n_far = jnp.maximum(i - 1, 0)
    buf_a = (sa_scr, mxa_scr)
    buf_b = (sb_scr, mxb_scr)
    logits(0, *buf_a)

    def far_pair(j):
        logits(j + 1, *buf_b)
        consume(j, *buf_a, None)
        logits(j + 2, *buf_a)
        consume(j + 1, *buf_b, None)

    def far_quad(t, carry):
        far_pair(4 * t)
        far_pair(4 * t + 2)
        return carry

    lax.fori_loop(0, n_far // 4, far_quad, 0)

    @pl.when(n_far % 4 >= 2)
    def _():
        far_pair(4 * (n_far // 4))

    @pl.when(i == 0)
    def _():
        consume(i, *buf_a, DIAG)

    @pl.when(jnp.logical_and(i > 0, n_far % 2 == 0))
    def _():
        logits(i, *buf_b)
        consume(i - 1, *buf_a, NEAR)
        consume(i, *buf_b, DIAG)

    @pl.when(n_far % 2 == 1)
    def _():
        logits(i - 1, *buf_b)
        consume(i - 2, *buf_a, None)
        logits(i, *buf_a)
        consume(i - 1, *buf_b, NEAR)
        consume(i, *buf_a, DIAG)

    lq = lamqk_ref[...]
    e1 = jnp.exp(jnp.sum(lq[0:1] * lq[1:2], axis=1, keepdims=True))
    e2 = jnp.exp(jnp.sum(lq[2:3] * lq[3:4], axis=1, keepdims=True))
    lam = e1 - e2 + lam_init

    o = acc_scr[:dv, :] / acc_scr[dv:dv + 1, :]
    ot = o[:, :bq] - lam * o[:, bq:]
    ms = jnp.mean(ot * ot, axis=0, keepdims=True)
    on = ot * lax.rsqrt(ms + EPS) * g_ref[...] * (1.0 - lam_init)
    o_ref[...] = on.T.astype(o_ref.dtype)


def _diff_attention(proj, vt_ext, bias_tiles, lam_qk, subln_col, lam_init):
    s = proj.shape[0]
    bq = min(ATT_BLOCK, s)
    nq = s // bq
    kcol = A_WIDTH // LANES
    vrows = vt_ext.shape[2]
    return pl.pallas_call(
        functools.partial(_attn_kernel, lam_init),
        grid=(A_HEADS, nq),
        in_specs=[
            pl.BlockSpec((bq, LANES), lambda h, i: (i, h)),
            pl.BlockSpec((s, LANES), lambda h, i: (0, kcol + h)),
            pl.BlockSpec((None, nq, vrows, bq), lambda h, i: (h, 0, 0, 0)),
            pl.BlockSpec((None, 2, bq, 2 * bq), lambda h, i: (h, 0, 0, 0)),
            pl.BlockSpec((4, A_HEAD_DIM), lambda h, i: (0, 0)),
            pl.BlockSpec((LANES, 1), lambda h, i: (0, 0)),
        ],
        out_specs=pl.BlockSpec((bq, LANES), lambda h, i: (i, h)),
        out_shape=jax.ShapeDtypeStruct((s, A_WIDTH), jnp.bfloat16),
        scratch_shapes=[
            pltpu.VMEM((LANES, 2 * bq), jnp.bfloat16),
            pltpu.VMEM((bq, 2 * bq), jnp.float32),
            pltpu.VMEM((bq, 2 * bq), jnp.float32),
            pltpu.VMEM((1, 2 * bq), jnp.float32),
            pltpu.VMEM((1, 2 * bq), jnp.float32),
            pltpu.VMEM((1, 2 * bq), jnp.float32),
            pltpu.VMEM((vrows, 2 * bq), jnp.float32),
        ],
        compiler_params=pltpu.CompilerParams(
            dimension_semantics=("arbitrary", "arbitrary"), vmem_limit_bytes=VMEM_LIMIT),
        name="diff_attention",
    )(proj, proj, vt_ext, bias_tiles, lam_qk, subln_col)


def _values_transposed(proj, s):
    bk = min(ATT_BLOCK, s)
    dv = 2 * A_HEAD_DIM
    v = proj[:, 2 * A_WIDTH:3 * A_WIDTH].reshape(s // bk, bk, A_HEADS, dv)
    vt = v.transpose(2, 0, 3, 1)
    extra = jnp.zeros((A_HEADS, s // bk, 16, bk), vt.dtype).at[:, :, 0, :].set(1.0)
    return jnp.concatenate([vt, extra], axis=2)


def _t5_bucket(dist):
    max_exact = NUM_BUCKETS // 2
    nf = jnp.maximum(dist, 1).astype(jnp.float32)
    large = max_exact + (jnp.log(nf / max_exact) / math.log(MAX_DISTANCE / max_exact)
                         * (NUM_BUCKETS - max_exact)).astype(jnp.int32)
    large = jnp.minimum(large, NUM_BUCKETS - 1)
    return jnp.where(dist < max_exact, dist, large)


def _attention_bias_tiles(rel_table, s):
    bq = min(ATT_BLOCK, s)
    dists = jnp.arange(2 * bq)
    bias = rel_table[_t5_bucket(dists)].T.astype(jnp.float32)
    far = rel_table[_t5_bucket(jnp.array([s - 1]))].T.astype(jnp.float32)
    bias = (bias - far) * LOG2E
    period = 2 * bq + 1
    masked = jnp.full((A_HEADS, bq + 1), NEG_BIG, jnp.float32)
    u_diag = jnp.concatenate([bias[:, :bq], masked], axis=1)
    u_near = jnp.concatenate([bias[:, bq:], jnp.zeros((A_HEADS, 1), jnp.float32), bias[:, :bq]], axis=1)
    u = jnp.stack([u_diag, u_near], axis=1)
    flat = jnp.tile(u, (1, 1, bq))[:, :, :bq * 2 * bq]
    tiles = flat.reshape(A_HEADS, 2, bq, 2 * bq)[..., :bq]
    return jnp.concatenate([tiles, tiles], axis=-1)


def _mlstm_head(hd, c, q_ref, k_ref, v_ref, o_ref, gate_ref, gbias_ref, cwq_ref, cwk_ref, cbq_ref, cbk_ref,
                hg_ref, y_ref, state_scr, m_scr, pq_scr, pk_scr):
    L = q_ref.shape[0]
    dh = M_HEAD_DIM

    @pl.when(c == 0)
    def _():
        for scr in (state_scr, m_scr, pq_scr, pk_scr):
            scr[...] = jnp.zeros(scr.shape, jnp.float32)

    def conv_silu(u_ref, hist_scr, w_ref, b_ref):
        u = u_ref[...].astype(jnp.float32)
        hist_scr[8:, :] = u
        w = w_ref[...]
        y = b_ref[...] + w[CONV_WIDTH - 1:CONV_WIDTH] * u
        for back in range(1, CONV_WIDTH):
            y = y + w[CONV_WIDTH - 1 - back:CONV_WIDTH - back] * hist_scr[8 - back:8 - back + L, :]
        hist_scr[:8, :] = u[L - 8:]
        half = 0.5 * y
        return half + half * jnp.tanh(half)

    qc = conv_silu(q_ref, pq_scr, cwq_ref, cbq_ref)
    kc = conv_silu(k_ref, pk_scr, cwk_ref, cbk_ref) * (dh ** -0.5)
    qb = qc.astype(jnp.bfloat16)
    kb = kc.astype(jnp.bfloat16)

    gb = gbias_ref[...]
    gates = gate_ref[...] + gb
    sub = lax.broadcasted_iota(jnp.int32, gates.shape, 0)
    ig_row = jnp.sum(jnp.where(sub == hd, gates, 0.0), axis=0, keepdims=True)
    fg_row = jnp.sum(jnp.where(sub == hd + M_HEADS, gates, 0.0), axis=0, keepdims=True)
    lf_row = jax.nn.log_sigmoid(fg_row)

    t_idx = lax.broadcasted_iota(jnp.int32, (L, L), 0)
    s_idx = lax.broadcasted_iota(jnp.int32, (L, L), 1)
    tril = s_idx <= t_idx
    eye = s_idx == t_idx

    def to_col(r):
        return jnp.sum(jnp.where(eye, r, 0.0), axis=1, keepdims=True)

    def to_row(col):
        return jnp.sum(jnp.where(eye, col, 0.0), axis=0, keepdims=True)

    bcum_col = jnp.sum(jnp.where(tril, lf_row, 0.0), axis=1, keepdims=True)
    bcum_row = to_row(bcum_col)
    ig_col = to_col(ig_row)
    b_last = bcum_col[L - 1:L, :]
    m_prev = m_scr[0:1, 0:1]

    dmat = jnp.where(tril, bcum_col - bcum_row + ig_row, NEG_BIG)
    inter = bcum_col + m_prev
    m_t = jnp.maximum(jnp.max(dmat, axis=1, keepdims=True), inter)
    qk = lax.dot_general(qb, kb, (((1,), (1,)), ((), ())), preferred_element_type=jnp.float32)
    scores = qk * jnp.exp(dmat - m_t)
    a_inter = jnp.exp(inter - m_t)

    lane_w = lax.broadcasted_iota(jnp.int32, (L, LANES), 1)
    v_ext = jnp.concatenate(
        [v_ref[...], jnp.where(lane_w == 0, 1.0, 0.0).astype(jnp.bfloat16)], axis=1)
    state = state_scr[...]
    res = _f32dot(scores.astype(jnp.bfloat16), v_ext) + a_inter * _f32dot(qb, state.astype(jnp.bfloat16))
    num = res[:, :dh]
    den = res[:, dh:dh + 1]
    h = num / jnp.maximum(jnp.abs(den), jnp.exp(-m_t))

    g_col = b_last - bcum_col + ig_col
    m_new = jnp.maximum(b_last + m_prev, jnp.max(g_col, axis=0, keepdims=True))
    wk = jnp.exp(g_col - m_new)
    decay = jnp.exp(b_last + m_prev - m_new)
    kw = (kc * wk).astype(jnp.bfloat16)
    upd = lax.dot_general(kw, v_ext, (((0,), (0,)), ((), ())), preferred_element_type=jnp.float32)
    state_scr[...] = decay * state + upd
    m_scr[...] = jnp.broadcast_to(m_new, m_scr.shape)

    hn = h * lax.rsqrt(jnp.mean(h * h, axis=-1, keepdims=True) + EPS) * hg_ref[...]
    o_gate = 0.5 + 0.5 * jnp.tanh(0.5 * o_ref[...].astype(jnp.float32))
    y_ref[...] = (o_gate * hn).astype(y_ref.dtype)


def _mlstm_kernel(q_ref, k_ref, v_ref, o_ref, gate_ref, gbias_ref, cwq_ref, cwk_ref, cbq_ref, cbk_ref,
                  hg_ref, y_ref, state_scr, m_scr, pq_scr, pk_scr):
    dh = M_HEAD_DIM
    for hh in range(M_HEADS_PER_STEP):
        cols = pl.ds(hh * dh, dh)
        _mlstm_head(pl.program_id(0) * M_HEADS_PER_STEP + hh, pl.program_id(1),
                    q_ref.at[:, cols], k_ref.at[:, cols], v_ref.at[:, cols], o_ref.at[:, cols],
                    gate_ref, gbias_ref, cwq_ref.at[:, cols], cwk_ref.at[:, cols],
                    cbq_ref.at[:, cols], cbk_ref.at[:, cols], hg_ref, y_ref.at[:, cols],
                    state_scr.at[hh], m_scr.at[hh], pq_scr.at[hh], pk_scr.at[hh])


def _mlstm(proj, gates_t, gate_bias, conv_w, conv_b, mhn_g):
    s = proj.shape[0]
    L = min(M_CHUNK, s)
    nc = s // L
    dh = M_HEAD_DIM
    hps = M_HEADS_PER_STEP
    w = hps * dh
    ng = M_HEADS // hps
    base = 3 * D_MODEL // w
    return pl.pallas_call(
        _mlstm_kernel,
        grid=(ng, nc),
        in_specs=[
            pl.BlockSpec((L, w), lambda g, c: (c, base + g)),
            pl.BlockSpec((L, w), lambda g, c: (c, base + ng + g)),
            pl.BlockSpec((L, w), lambda g, c: (c, base + 2 * ng + g)),
            pl.BlockSpec((L, w), lambda g, c: (c, base + 3 * ng + g)),
            pl.BlockSpec((2 * M_HEADS, L), lambda g, c: (0, c)),
            pl.BlockSpec((2 * M_HEADS, 1), lambda g, c: (0, 0)),
            pl.BlockSpec((CONV_WIDTH, w), lambda g, c: (0, g)),
            pl.BlockSpec((CONV_WIDTH, w), lambda g, c: (0, ng + g)),
            pl.BlockSpec((1, w), lambda g, c: (0, g)),
            pl.BlockSpec((1, w), lambda g, c: (0, ng + g)),
            pl.BlockSpec((1, dh), lambda g, c: (0, 0)),
        ],
        out_specs=pl.BlockSpec((L, w), lambda g, c: (c, g)),
        out_shape=jax.ShapeDtypeStruct((s, D_MODEL), jnp.bfloat16),
        scratch_shapes=[
            pltpu.VMEM((hps, dh, STATE_W), jnp.float32),
            pltpu.VMEM((hps, 8, LANES), jnp.float32),
            pltpu.VMEM((hps, L + 8, dh), jnp.float32),
            pltpu.VMEM((hps, L + 8, dh), jnp.float32),
        ],
        compiler_params=pltpu.CompilerParams(
            dimension_semantics=("arbitrary", "arbitrary"), vmem_limit_bytes=VMEM_LIMIT),
        name="mlstm",
    )(proj, proj, proj, proj, gates_t, gate_bias, conv_w, conv_w, conv_b, conv_b, mhn_g)


def _merge_kernel(x_ref, ya_ref, ym_ref, ga_ref, gm_ref, wa_ref, wm_ref, wo_ref, gt_ref, o_ref):
    a = _f32dot(ya_ref[...], wa_ref[...])
    b = _f32dot(ym_ref[...], wm_ref[...])
    merged = (jax.nn.sigmoid(ga_ref[...].astype(jnp.float32)) * a
              + jax.nn.sigmoid(gm_ref[...].astype(jnp.float32)) * b)
    o_ref[...] = x_ref[...] + gt_ref[...] * _f32dot(merged.astype(jnp.bfloat16), wo_ref[...])


def _merge(x, ya, ym, proj, w_a, w_m, w_out, gt):
    s = x.shape[0]
    tm = min(MLP_TILE, s)
    row = lambda i: (i, 0)
    const = lambda i: (0, 0)
    wspec = pl.BlockSpec((D_MODEL, D_MODEL), const, pipeline_mode=pl.Buffered(1))
    return pl.pallas_call(
        _merge_kernel,
        grid=(s // tm,),
        in_specs=[
            pl.BlockSpec((tm, D_MODEL), row),
            pl.BlockSpec((tm, D_MODEL), row),
            pl.BlockSpec((tm, D_MODEL), row),
            pl.BlockSpec((tm, D_MODEL), lambda i: (i, 7)),
            pl.BlockSpec((tm, D_MODEL), lambda i: (i, 8)),
            wspec, wspec, wspec,
            pl.BlockSpec((1, D_MODEL), const),
        ],
        out_specs=pl.BlockSpec((tm, D_MODEL), row),
        out_shape=jax.ShapeDtypeStruct((s, D_MODEL), jnp.float32),
        compiler_params=pltpu.CompilerParams(
            dimension_semantics=("arbitrary",), vmem_limit_bytes=VMEM_LIMIT),
        name="merge_out_projection",
    )(x, ya, ym, proj, proj, w_a, w_m, w_out, gt)


def _mlp_kernel(x_ref, g_ref, sc_ref, sh_ref, gt_ref, w1_ref, w2_ref, o_ref):
    x = x_ref[...]
    hb = _modulated_norm(x, g_ref[...], sc_ref[...], sh_ref[...]).astype(jnp.bfloat16)
    acc = None
    for t in range(D_FF // D_MODEL):
        u = jnp.maximum(_f32dot(hb, w1_ref[:, t * D_MODEL:(t + 1) * D_MODEL]), 0.0)
        part = _f32dot((u * u).astype(jnp.bfloat16), w2_ref[t * D_MODEL:(t + 1) * D_MODEL, :])
        acc = part if acc is None else acc + part
    o_ref[...] = x + gt_ref[...] * acc


def _mlp(x, g, sc, sh, gt, w1, w2):
    s = x.shape[0]
    tm = min(MLP_TILE, s)
    row = lambda i: (i, 0)
    const = lambda i: (0, 0)
    vec = pl.BlockSpec((1, D_MODEL), const)
    return pl.pallas_call(
        _mlp_kernel,
        grid=(s // tm,),
        in_specs=[
            pl.BlockSpec((tm, D_MODEL), row),
            vec, vec, vec, vec,
            pl.BlockSpec((D_MODEL, D_FF), const, pipeline_mode=pl.Buffered(1)),
            pl.BlockSpec((D_FF, D_MODEL), const, pipeline_mode=pl.Buffered(1)),
        ],
        out_specs=pl.BlockSpec((tm, D_MODEL), row),
        out_shape=jax.ShapeDtypeStruct((s, D_MODEL), jnp.float32),
        compiler_params=pltpu.CompilerParams(
            dimension_semantics=("arbitrary",), vmem_limit_bytes=VMEM_LIMIT),
        name="relu2_mlp",
    )(x, g, sc, sh, gt, w1, w2)


def kernel(x, c, w_ada, b_ada, norm_mix_g, norm_ffn_g, w_in, b_igate, b_fgate, qn_g, kn_g, lam_qk,
           subln_g, rel_table, conv_w, conv_b, mhn_g, w_a, w_m, w_out, w_ff1, w_ff2):
    b, s, d = x.shape
    assert b == 1 and d == D_MODEL
    depth = w_ada.shape[0]
    bf = jnp.bfloat16
    xs = x[0]

    mod = _ada_modulation(c, w_ada, b_ada)
    bias_tiles = _attention_bias_tiles(rel_table, s)
    blk = A_HEAD_DIM
    bd = (jnp.arange(256)[:, None] // blk == jnp.arange(256)[None, :] // blk).astype(bf)

    gate_lo = 7 * D_MODEL
    gate_hi = gate_lo + 2 * M_HEADS
    for l in range(depth):
        lam_init = 0.8 - 0.6 * math.exp(-0.3 * l)
        sh_a, sc_a, gt_a, sh_f, sc_f, gt_f = [mod[l:l + 1, t * d:(t + 1) * d] for t in range(6)]

        w = w_in[l]
        w_main = jnp.concatenate([w[:, :gate_lo], w[:, gate_hi:]], axis=1).astype(bf)
        w_gate = jnp.pad(w[:, gate_lo:gate_hi], ((0, 0), (0, GATE_PAD - 2 * M_HEADS))).astype(bf)
        reps = A_WIDTH // A_HEAD_DIM
        qk_gain = jnp.stack([jnp.tile(qn_g[l], reps) * (A_HEAD_DIM ** -0.5 * LOG2E),
                             jnp.tile(kn_g[l], reps)])[:, None, :]

        proj, gates = _in_projection(xs, norm_mix_g[l][None], sc_a, sh_a, w_main, w_gate, qk_gain, bd)

        ya = _diff_attention(proj, _values_transposed(proj, s), bias_tiles, lam_qk[l],
                             subln_g[l][:, None], lam_init)

        gates_t = gates[:, :2 * M_HEADS].T
        gate_bias = jnp.concatenate([b_igate[l], b_fgate[l]])[:, None]
        ym = _mlstm(proj, gates_t, gate_bias, conv_w[l], conv_b[l][None], mhn_g[l][None])

        xs = _merge(xs, ya, ym, proj, w_a[l].astype(bf), w_m[l].astype(bf), w_out[l].astype(bf), gt_a)
        xs = _mlp(xs, norm_ffn_g[l][None], sc_f, sh_f, gt_f, w_ff1[l].astype(bf), w_ff2[l].astype(bf))
    return xs[None]
```

```python
import functools
import math

import jax
import jax.numpy as jnp
from jax import lax
from jax.experimental import pallas as pl
from jax.experimental.pallas import tpu as pltpu

D_MODEL = 1024
A_HEADS = 8
A_HEAD_DIM = 64
A_WIDTH = A_HEADS * 2 * A_HEAD_DIM
M_HEADS = 4
M_HEAD_DIM = D_MODEL // M_HEADS
CONV_WIDTH = 4
NUM_BUCKETS = 32
MAX_DISTANCE = 128
D_FF = 4 * D_MODEL
EPS = 1e-6

LANES = 128
GATE_PAD = LANES
N_ROWMAJOR = 7 * D_MODEL
VMEM_LIMIT = 56 * 1024 * 1024
NEG_BIG = -1e30
LOG2E = math.log2(math.e)

ATT_BLOCK = 512
M_CHUNK = 256
M_HEADS_PER_STEP = 2
MLP_TILE = 512
STATE_W = M_HEAD_DIM + LANES


def _f32dot(a, b):
    return jnp.dot(a, b, preferred_element_type=jnp.float32)


def _modulated_norm(x, g, sc, sh):
    y = x * lax.rsqrt(jnp.mean(x * x, axis=-1, keepdims=True) + EPS)
    return (y * g) * (1.0 + sc) + sh


def _ada_kernel(c_ref, w_ref, b_ref, o_ref):
    c = c_ref[...]
    cond = c * jax.nn.sigmoid(c)
    o_ref[...] = jnp.sum(cond * w_ref[...], axis=0, keepdims=True) + b_ref[...]


def _ada_modulation(c, w_ada, b_ada):
    depth = w_ada.shape[0]
    nblk = w_ada.shape[2] // D_MODEL
    out = pl.pallas_call(
        _ada_kernel,
        grid=(depth, nblk),
        in_specs=[
            pl.BlockSpec((D_MODEL, 1), lambda l, j: (0, 0)),
            pl.BlockSpec((None, D_MODEL, D_MODEL), lambda l, j: (l, 0, j)),
            pl.BlockSpec((None, 1, D_MODEL), lambda l, j: (l, 0, j)),
        ],
        out_specs=pl.BlockSpec((None, 1, D_MODEL), lambda l, j: (l, 0, j)),
        out_shape=jax.ShapeDtypeStruct((depth, 1, w_ada.shape[2]), jnp.float32),
        name="ada_modulation",
    )(c.reshape(D_MODEL, 1), w_ada, b_ada[:, None, :])
    return out[:, 0, :]


def _proj_kernel(x_ref, g_ref, sc_ref, sh_ref, w_ref, wg_ref, qkg_ref, bd_ref,
                 o_ref, gate_ref, qzt_ref, vt_ref):
    tm = x_ref.shape[0]
    dh = A_HEAD_DIM
    h = _modulated_norm(x_ref[...], g_ref[...], sc_ref[...], sh_ref[...])
    hb = h.astype(jnp.bfloat16)
    gate_ref[...] = _f32dot(hb, wg_ref[...])
    bd = bd_ref[...]
    lane = lax.broadcasted_iota(jnp.int32, (tm, LANES), 1)
    for j in range(w_ref.shape[1] // D_MODEL):
        y = _f32dot(hb, w_ref[:, j * D_MODEL:(j + 1) * D_MODEL])
        if j == 2:
            for hd in range(A_HEADS):
                vt_ref[hd, 0] = y[:, hd * LANES:(hd + 1) * LANES].T.astype(vt_ref.dtype)
        elif j > 2:
            o_ref[:, (j - 2) * D_MODEL:(j - 1) * D_MODEL] = y.astype(o_ref.dtype)
        else:
            for t in range(D_MODEL // 256):
                yt = y[:, t * 256:(t + 1) * 256]
                ss = _f32dot((yt * yt).astype(jnp.bfloat16), bd)
                yn = yt * lax.rsqrt(ss * (1.0 / A_HEAD_DIM) + EPS) * qkg_ref[j, :, t * 256:(t + 1) * 256]
                if j == 1:
                    o_ref[:, t * 256:(t + 1) * 256] = yn.astype(o_ref.dtype)
                    continue
                for hh in range(256 // LANES):
                    hd = t * (256 // LANES) + hh
                    qh = yn[:, hh * LANES:(hh + 1) * LANES]
                    qzt_ref[hd, 0, :, :tm] = jnp.where(lane < dh, qh, 0.0).T.astype(qzt_ref.dtype)
                    qzt_ref[hd, 0, :, tm:] = jnp.where(lane >= dh, qh, 0.0).T.astype(qzt_ref.dtype)


def _in_projection(x, g, sc, sh, w_main, w_gate, qk_gain, bd):
    s = x.shape[0]
    tm = min(ATT_BLOCK, s)
    nt = s // tm
    const = lambda i: (0, 0)
    vec = pl.BlockSpec((1, D_MODEL), const)
    resident = pl.Buffered(1)
    return pl.pallas_call(
        _proj_kernel,
        grid=(nt,),
        in_specs=[
            pl.BlockSpec((tm, D_MODEL), lambda i: (i, 0)),
            vec, vec, vec,
            pl.BlockSpec((D_MODEL, w_main.shape[1]), const, pipeline_mode=resident),
            pl.BlockSpec((D_MODEL, GATE_PAD), const, pipeline_mode=resident),
            pl.BlockSpec((2, 1, D_MODEL), lambda i: (0, 0, 0)),
            pl.BlockSpec((256, 256), const),
        ],
        out_specs=[
            pl.BlockSpec((tm, N_ROWMAJOR), lambda i: (i, 0)),
            pl.BlockSpec((tm, GATE_PAD), lambda i: (i, 0)),
            pl.BlockSpec((A_HEADS, 1, LANES, 2 * tm), lambda i: (0, i, 0, 0)),
            pl.BlockSpec((A_HEADS, 1, LANES, tm), lambda i: (0, i, 0, 0)),
        ],
        out_shape=[
            jax.ShapeDtypeStruct((s, N_ROWMAJOR), jnp.bfloat16),
            jax.ShapeDtypeStruct((s, GATE_PAD), jnp.float32),
            jax.ShapeDtypeStruct((A_HEADS, nt, LANES, 2 * tm), jnp.bfloat16),
            jax.ShapeDtypeStruct((A_HEADS, nt, LANES, tm), jnp.bfloat16),
        ],
        compiler_params=pltpu.CompilerParams(
            dimension_semantics=("arbitrary",), vmem_limit_bytes=VMEM_LIMIT),
        name="in_projection",
    )(x, g, sc, sh, w_main, w_gate, qk_gain, bd)


def _attn_kernel(lam_init, qzt_ref, k_ref, vt_ref, ones_ref, bias_ref, lamqk_ref, g_ref, o_ref,
                 sa_scr, sb_scr, mxa_scr, mxb_scr, m_scr, acc_scr):
    i = pl.program_id(1)
    bq = qzt_ref.shape[1] // 2
    bk = bq
    dv = 2 * A_HEAD_DIM

    m_scr[...] = jnp.full(m_scr.shape, NEG_BIG, jnp.float32)
    acc_scr[...] = jnp.zeros(acc_scr.shape, jnp.float32)

    def logits(j, s_ref, mx_ref):
        kb = k_ref[pl.ds(pl.multiple_of(j * bk, bk), bk), :]
        st = _f32dot(kb, qzt_ref[...])
        s_ref[...] = st
        mx_ref[...] = jnp.max(st, axis=0, keepdims=True)

    def consume(j, s_ref, mx_ref, bias_idx):
        st = s_ref[...]
        if bias_idx is None:
            mx = mx_ref[...]
        else:
            tile = bias_ref[bias_idx]
            st = st + jnp.concatenate([tile, tile], axis=1)
            mx = jnp.max(st, axis=0, keepdims=True)
        m_old = m_scr[...]
        m_new = jnp.maximum(m_old, mx)
        pt = jnp.exp2(st - m_new).astype(jnp.bfloat16)
        vt = jnp.concatenate([vt_ref[j], ones_ref[...]], axis=0)
        pv = _f32dot(vt, pt)
        acc_scr[...] = acc_scr[...] * jnp.exp2(m_old - m_new) + pv
        m_scr[...] = m_new

    NEAR, DIAG = 1, 0
    n_far = jnp.maximum(i - 1, 0)
    buf_a = (sa_scr, mxa_scr)
    buf_b = (sb_scr, mxb_scr)
    logits(0, *buf_a)

    def far_pair(j):
        logits(j + 1, *buf_b)
        consume(j, *buf_a, None)
        logits(j + 2, *buf_a)
        consume(j + 1, *buf_b, None)

    def far_quad(t, carry):
        far_pair(4 * t)
        far_pair(4 * t + 2)
        return carry

    lax.fori_loop(0, n_far // 4, far_quad, 0)

    @pl.when(n_far % 4 >= 2)
    def _():
        far_pair(4 * (n_far // 4))

    @pl.when(i == 0)
    def _():
        consume(i, *buf_a, DIAG)

    @pl.when(jnp.logical_and(i > 0, n_far % 2 == 0))
    def _():
        logits(i, *buf_b)
        consume(i - 1, *buf_a, NEAR)
        consume(i, *buf_b, DIAG)

    @pl.when(n_far % 2 == 1)
    def _():
        logits(i - 1, *buf_b)
        consume(i - 2, *buf_a, None)
        logits(i, *buf_a)
        consume(i - 1, *buf_b, NEAR)
        consume(i, *buf_a, DIAG)

    lq = lamqk_ref[...]
    e1 = jnp.exp(jnp.sum(lq[0:1] * lq[1:2], axis=1, keepdims=True))
    e2 = jnp.exp(jnp.sum(lq[2:3] * lq[3:4], axis=1, keepdims=True))
    lam = e1 - e2 + lam_init

    o = acc_scr[:dv, :] * (1.0 / acc_scr[dv:dv + 1, :])
    ot = o[:, :bq] - lam * o[:, bq:]
    ms = jnp.mean(ot * ot, axis=0, keepdims=True)
    on = ot * lax.rsqrt(ms + EPS) * g_ref[...] * (1.0 - lam_init)
    o_ref[...] = on.T.astype(o_ref.dtype)


def _diff_attention(qzt, proj, vt, bias_tiles, lam_qk, subln_col, lam_init):
    s = proj.shape[0]
    bq = min(ATT_BLOCK, s)
    nq = s // bq
    pad_rows = 16
    ones_rows = jnp.zeros((pad_rows, bq), jnp.bfloat16).at[0].set(1.0)
    return pl.pallas_call(
        functools.partial(_attn_kernel, lam_init),
        grid=(A_HEADS, nq),
        in_specs=[
            pl.BlockSpec((None, None, LANES, 2 * bq), lambda h, i: (h, i, 0, 0)),
            pl.BlockSpec((s, LANES), lambda h, i: (0, h)),
            pl.BlockSpec((None, nq, LANES, bq), lambda h, i: (h, 0, 0, 0)),
            pl.BlockSpec((pad_rows, bq), lambda h, i: (0, 0)),
            pl.BlockSpec((None, 2, bq, bq), lambda h, i: (h, 0, 0, 0)),
            pl.BlockSpec((4, A_HEAD_DIM), lambda h, i: (0, 0)),
            pl.BlockSpec((LANES, 1), lambda h, i: (0, 0)),
        ],
        out_specs=pl.BlockSpec((bq, LANES), lambda h, i: (i, h)),
        out_shape=jax.ShapeDtypeStruct((s, A_WIDTH), jnp.bfloat16),
        scratch_shapes=[
            pltpu.VMEM((bq, 2 * bq), jnp.float32),
            pltpu.VMEM((bq, 2 * bq), jnp.float32),
            pltpu.VMEM((1, 2 * bq), jnp.float32),
            pltpu.VMEM((1, 2 * bq), jnp.float32),
            pltpu.VMEM((1, 2 * bq), jnp.float32),
            pltpu.VMEM((LANES + pad_rows, 2 * bq), jnp.float32),
        ],
        compiler_params=pltpu.CompilerParams(
            dimension_semantics=("arbitrary", "arbitrary"), vmem_limit_bytes=VMEM_LIMIT),
        name="diff_attention",
    )(qzt, proj, vt, ones_rows, bias_tiles, lam_qk, subln_col)


def _t5_bucket(dist):
    max_exact = NUM_BUCKETS // 2
    nf = jnp.maximum(dist, 1).astype(jnp.float32)
    large = max_exact + (jnp.log(nf / max_exact) / math.log(MAX_DISTANCE / max_exact)
                         * (NUM_BUCKETS - max_exact)).astype(jnp.int32)
    large = jnp.minimum(large, NUM_BUCKETS - 1)
    return jnp.where(dist < max_exact, dist, large)


def _attention_bias_tiles(rel_table, s):
    bq = min(ATT_BLOCK, s)
    dists = jnp.arange(2 * bq)
    bias = rel_table[_t5_bucket(dists)].T.astype(jnp.float32)
    far = rel_table[_t5_bucket(jnp.array([s - 1]))].T.astype(jnp.float32)
    bias = (bias - far) * LOG2E
    period = 2 * bq + 1
    masked = jnp.full((A_HEADS, bq + 1), NEG_BIG, jnp.float32)
    u_diag = jnp.concatenate([bias[:, :bq], masked], axis=1)
    u_near = jnp.concatenate([bias[:, bq:], jnp.zeros((A_HEADS, 1), jnp.float32), bias[:, :bq]], axis=1)
    u = jnp.stack([u_diag, u_near], axis=1)
    flat = jnp.tile(u, (1, 1, bq))[:, :, :bq * 2 * bq]
    return flat.reshape(A_HEADS, 2, bq, 2 * bq)[..., :bq]


def _mlstm_head(hd, c, q_ref, k_ref, v_ref, o_ref, gate_ref, gbias_ref, cwq_ref, cwk_ref, cbq_ref, cbk_ref,
                hg_ref, y_ref, state_scr, m_scr, pq_scr, pk_scr):
    L = q_ref.shape[0]
    dh = M_HEAD_DIM

    @pl.when(c == 0)
    def _():
        for scr in (state_scr, m_scr, pq_scr, pk_scr):
            scr[...] = jnp.zeros(scr.shape, jnp.float32)

    def conv_silu(u_ref, hist_scr, w_ref, b_ref):
        u = u_ref[...].astype(jnp.float32)
        hist_scr[8:, :] = u
        w = w_ref[...]
        y = b_ref[...] + w[CONV_WIDTH - 1:CONV_WIDTH] * u
        for back in range(1, CONV_WIDTH):
            y = y + w[CONV_WIDTH - 1 - back:CONV_WIDTH - back] * hist_scr[8 - back:8 - back + L, :]
        hist_scr[:8, :] = u[L - 8:]
        half = 0.5 * y
        return half + half * jnp.tanh(half)

    qc = conv_silu(q_ref, pq_scr, cwq_ref, cbq_ref)
    kc = conv_silu(k_ref, pk_scr, cwk_ref, cbk_ref) * (dh ** -0.5)
    qb = qc.astype(jnp.bfloat16)
    kb = kc.astype(jnp.bfloat16)

    gb = gbias_ref[...]
    gates = gate_ref[...] + gb
    sub = lax.broadcasted_iota(jnp.int32, gates.shape, 0)
    ig_row = jnp.sum(jnp.where(sub == hd, gates, 0.0), axis=0, keepdims=True)
    fg_row = jnp.sum(jnp.where(sub == hd + M_HEADS, gates, 0.0), axis=0, keepdims=True)
    lf_row = jax.nn.log_sigmoid(fg_row)

    t_idx = lax.broadcasted_iota(jnp.int32, (L, L), 0)
    s_idx = lax.broadcasted_iota(jnp.int32, (L, L), 1)
    tril = s_idx <= t_idx
    eye = s_idx == t_idx

    def to_col(r):
        return jnp.sum(jnp.where(eye, r, 0.0), axis=1, keepdims=True)

    def to_row(col):
        return jnp.sum(jnp.where(eye, col, 0.0), axis=0, keepdims=True)

    bcum_col = jnp.sum(jnp.where(tril, lf_row, 0.0), axis=1, keepdims=True)
    bcum_row = to_row(bcum_col)
    ig_col = to_col(ig_row)
    b_last = bcum_col[L - 1:L, :]
    m_prev = m_scr[0:1, 0:1]

    dmat = jnp.where(tril, bcum_col - bcum_row + ig_row, NEG_BIG)
    inter = bcum_col + m_prev
    m_t = jnp.maximum(jnp.max(dmat, axis=1, keepdims=True), inter)
    qk = lax.dot_general(qb, kb, (((1,), (1,)), ((), ())), preferred_element_type=jnp.float32)
    scores = qk * jnp.exp(dmat - m_t)
    a_inter = jnp.exp(inter - m_t)

    lane_w = lax.broadcasted_iota(jnp.int32, (L, LANES), 1)
    v_ext = jnp.concatenate(
        [v_ref[...], jnp.where(lane_w == 0, 1.0, 0.0).astype(jnp.bfloat16)], axis=1)
    state = state_scr[...]
    res = _f32dot(scores.astype(jnp.bfloat16), v_ext) + a_inter * _f32dot(qb, state.astype(jnp.bfloat16))
    num = res[:, :dh]
    den = res[:, dh:dh + 1]
    h = num / jnp.maximum(jnp.abs(den), jnp.exp(-m_t))

    g_col = b_last - bcum_col + ig_col
    m_new = jnp.maximum(b_last + m_prev, jnp.max(g_col, axis=0, keepdims=True))
    wk = jnp.exp(g_col - m_new)
    decay = jnp.exp(b_last + m_prev - m_new)
    kw = (kc * wk).astype(jnp.bfloat16)
    upd = lax.dot_general(kw, v_ext, (((0,), (0,)), ((), ())), preferred_element_type=jnp.float32)
    state_scr[...] = decay * state + upd
    m_scr[...] = jnp.broadcast_to(m_new, m_scr.shape)

    hn = h * lax.rsqrt(jnp.mean(h * h, axis=-1, keepdims=True) + EPS) * hg_ref[...]
    o_gate = 0.5 + 0.5 * jnp.tanh(0.5 * o_ref[...].astype(jnp.float32))
    y_ref[...] = (o_gate * hn).astype(y_ref.dtype)


def _mlstm_kernel(q_ref, k_ref, v_ref, o_ref, gate_ref, gbias_ref, cwq_ref, cwk_ref, cbq_ref, cbk_ref,
                  hg_ref, y_ref, state_scr, m_scr, pq_scr, pk_scr):
    dh = M_HEAD_DIM
    for hh in range(M_HEADS_PER_STEP):
        cols = pl.ds(hh * dh, dh)
        _mlstm_head(pl.program_id(0) * M_HEADS_PER_STEP + hh, pl.program_id(1),
                    q_ref.at[:, cols], k_ref.at[:, cols], v_ref.at[:, cols], o_ref.at[:, cols],
                    gate_ref, gbias_ref, cwq_ref.at[:, cols], cwk_ref.at[:, cols],
                    cbq_ref.at[:, cols], cbk_ref.at[:, cols], hg_ref, y_ref.at[:, cols],
                    state_scr.at[hh], m_scr.at[hh], pq_scr.at[hh], pk_scr.at[hh])


def _mlstm(proj, gates_t, gate_bias, conv_w, conv_b, mhn_g):
    s = proj.shape[0]
    L = min(M_CHUNK, s)
    nc = s // L
    dh = M_HEAD_DIM
    hps = M_HEADS_PER_STEP
    w = hps * dh
    ng = M_HEADS // hps
    base = D_MODEL // w
    return pl.pallas_call(
        _mlstm_kernel,
        grid=(ng, nc),
        in_specs=[
            pl.BlockSpec((L, w), lambda g, c: (c, base + g)),
            pl.BlockSpec((L, w), lambda g, c: (c, base + ng + g)),
            pl.BlockSpec((L, w), lambda g, c: (c, base + 2 * ng + g)),
            pl.BlockSpec((L, w), lambda g, c: (c, base + 3 * ng + g)),
            pl.BlockSpec((2 * M_HEADS, L), lambda g, c: (0, c)),
            pl.BlockSpec((2 * M_HEADS, 1), lambda g, c: (0, 0)),
            pl.BlockSpec((CONV_WIDTH, w), lambda g, c: (0, g)),
            pl.BlockSpec((CONV_WIDTH, w), lambda g, c: (0, ng + g)),
            pl.BlockSpec((1, w), lambda g, c: (0, g)),
            pl.BlockSpec((1, w), lambda g, c: (0, ng + g)),
            pl.BlockSpec((1, dh), lambda g, c: (0, 0)),
        ],
        out_specs=pl.BlockSpec((L, w), lambda g, c: (c, g)),
        out_shape=jax.ShapeDtypeStruct((s, D_MODEL), jnp.bfloat16),
        scratch_shapes=[
            pltpu.VMEM((hps, dh, STATE_W), jnp.float32),
            pltpu.VMEM((hps, 8, LANES), jnp.float32),
            pltpu.VMEM((hps, L + 8, dh), jnp.float32),
            pltpu.VMEM((hps, L + 8, dh), jnp.float32),
        ],
        compiler_params=pltpu.CompilerParams(
            dimension_semantics=("arbitrary", "arbitrary"), vmem_limit_bytes=VMEM_LIMIT),
        name="mlstm",
    )(proj, proj, proj, proj, gates_t, gate_bias, conv_w, conv_w, conv_b, conv_b, mhn_g)


def _merge_kernel(x_ref, ya_ref, ym_ref, ga_ref, gm_ref, wa_ref, wm_ref, wo_ref, gt_ref, o_ref):
    a = _f32dot(ya_ref[...], wa_ref[...])
    b = _f32dot(ym_ref[...], wm_ref[...])
    merged = (jax.nn.sigmoid(ga_ref[...].astype(jnp.float32)) * a
              + jax.nn.sigmoid(gm_ref[...].astype(jnp.float32)) * b)
    o_ref[...] = x_ref[...] + gt_ref[...] * _f32dot(merged.astype(jnp.bfloat16), wo_ref[...])


def _merge(x, ya, ym, proj, w_a, w_m, w_out, gt):
    s = x.shape[0]
    tm = min(MLP_TILE, s)
    row = lambda i: (i, 0)
    const = lambda i: (0, 0)
    wspec = pl.BlockSpec((D_MODEL, D_MODEL), const, pipeline_mode=pl.Buffered(1))
    return pl.pallas_call(
        _merge_kernel,
        grid=(s // tm,),
        in_specs=[
            pl.BlockSpec((tm, D_MODEL), row),
            pl.BlockSpec((tm, D_MODEL), row),
            pl.BlockSpec((tm, D_MODEL), row),
            pl.BlockSpec((tm, D_MODEL), lambda i: (i, 5)),
            pl.BlockSpec((tm, D_MODEL), lambda i: (i, 6)),
            wspec, wspec, wspec,
            pl.BlockSpec((1, D_MODEL), const),
        ],
        out_specs=pl.BlockSpec((tm, D_MODEL), row),
        out_shape=jax.ShapeDtypeStruct((s, D_MODEL), jnp.float32),
        compiler_params=pltpu.CompilerParams(
            dimension_semantics=("arbitrary",), vmem_limit_bytes=VMEM_LIMIT),
        name="merge_out_projection",
    )(x, ya, ym, proj, proj, w_a, w_m, w_out, gt)


def _mlp_kernel(x_ref, g_ref, sc_ref, sh_ref, gt_ref, w1_ref, w2_ref, o_ref):
    x = x_ref[...]
    hb = _modulated_norm(x, g_ref[...], sc_ref[...], sh_ref[...]).astype(jnp.bfloat16)
    acc = None
    for t in range(D_FF // D_MODEL):
        u = jnp.maximum(_f32dot(hb, w1_ref[:, t * D_MODEL:(t + 1) * D_MODEL]), 0.0)
        part = _f32dot((u * u).astype(jnp.bfloat16), w2_ref[t * D_MODEL:(t + 1) * D_MODEL, :])
        acc = part if acc is None else acc + part
    o_ref[...] = x + gt_ref[...] * acc


def _mlp(x, g, sc, sh, gt, w1, w2):
    s = x.shape[0]
    tm = min(MLP_TILE, s)
    row = lambda i: (i, 0)
    const = lambda i: (0, 0)
    vec = pl.BlockSpec((1, D_MODEL), const)
    return pl.pallas_call(
        _mlp_kernel,
        grid=(s // tm,),
        in_specs=[
            pl.BlockSpec((tm, D_MODEL), row),
            vec, vec, vec, vec,
            pl.BlockSpec((D_MODEL, D_FF), const, pipeline_mode=pl.Buffered(1)),
            pl.BlockSpec((D_FF, D_MODEL), const, pipeline_mode=pl.Buffered(1)),
        ],
        out_specs=pl.BlockSpec((tm, D_MODEL), row),
        out_shape=jax.ShapeDtypeStruct((s, D_MODEL), jnp.float32),
        compiler_params=pltpu.CompilerParams(
            dimension_semantics=("arbitrary",), vmem_limit_bytes=VMEM_LIMIT),
        name="relu2_mlp",
    )(x, g, sc, sh, gt, w1, w2)


def kernel(x, c, w_ada, b_ada, norm_mix_g, norm_ffn_g, w_in, b_igate, b_fgate, qn_g, kn_g, lam_qk,
           subln_g, rel_table, conv_w, conv_b, mhn_g, w_a, w_m, w_out, w_ff1, w_ff2):
    b, s, d = x.shape
    assert b == 1 and d == D_MODEL
    depth = w_ada.shape[0]
    bf = jnp.bfloat16
    xs = x[0]

    mod = _ada_modulation(c, w_ada, b_ada)
    bias_tiles = _attention_bias_tiles(rel_table, s)
    blk = A_HEAD_DIM
    bd = (jnp.arange(256)[:, None] // blk == jnp.arange(256)[None, :] // blk).astype(bf)

    gate_lo = 7 * D_MODEL
    gate_hi = gate_lo + 2 * M_HEADS
    for l in range(depth):
        lam_init = 0.8 - 0.6 * math.exp(-0.3 * l)
        sh_a, sc_a, gt_a, sh_f, sc_f, gt_f = [mod[l:l + 1, t * d:(t + 1) * d] for t in range(6)]

        w = w_in[l]
        w_main = jnp.concatenate([w[:, :gate_lo], w[:, gate_hi:]], axis=1).astype(bf)
        w_gate = jnp.pad(w[:, gate_lo:gate_hi], ((0, 0), (0, GATE_PAD - 2 * M_HEADS))).astype(bf)
        reps = A_WIDTH // A_HEAD_DIM
        qk_gain = jnp.stack([jnp.tile(qn_g[l], reps) * (A_HEAD_DIM ** -0.5 * LOG2E),
                             jnp.tile(kn_g[l], reps)])[:, None, :]

        proj, gates, qzt, vt = _in_projection(xs, norm_mix_g[l][None], sc_a, sh_a, w_main, w_gate,
                                              qk_gain, bd)

        ya = _diff_attention(qzt, proj, vt, bias_tiles, lam_qk[l], subln_g[l][:, None], lam_init)

        gates_t = gates[:, :2 * M_HEADS].T
        gate_bias = jnp.concatenate([b_igate[l], b_fgate[l]])[:, None]
        ym = _mlstm(proj, gates_t, gate_bias, conv_w[l], conv_b[l][None], mhn_g[l][None])

        xs = _merge(xs, ya, ym, proj, w_a[l].astype(bf), w_m[l].astype(bf), w_out[l].astype(bf), gt_a)
        xs = _mlp(xs, norm_ffn_g[l][None], sc_f, sh_f, gt_f, w_ff1[l].astype(bf), w_ff2[l].astype(bf))
    return xs[None]
```

```python
import functools
import math

import jax
import jax.numpy as jnp
from jax import lax
from jax.experimental import pallas as pl
from jax.experimental.pallas import tpu as pltpu

D_MODEL = 1024
A_HEADS = 8
A_HEAD_DIM = 64
A_WIDTH = A_HEADS * 2 * A_HEAD_DIM
M_HEADS = 4
M_HEAD_DIM = D_MODEL // M_HEADS
CONV_WIDTH = 4
NUM_BUCKETS = 32
MAX_DISTANCE = 128
D_FF = 4 * D_MODEL
EPS = 1e-6

LANES = 128
GATE_PAD = LANES
N_ROWMAJOR = 7 * D_MODEL
VMEM_LIMIT = 56 * 1024 * 1024
NEG_BIG = -1e30
LOG2E = math.log2(math.e)

ATT_BLOCK = 512
M_CHUNK = 256
M_HEADS_PER_STEP = 2
MLP_TILE = 512
STATE_W = M_HEAD_DIM + LANES


def _f32dot(a, b):
    return jnp.dot(a, b, preferred_element_type=jnp.float32)


def _modulated_norm(x, g, sc, sh):
    y = x * lax.rsqrt(jnp.mean(x * x, axis=-1, keepdims=True) + EPS)
    return (y * g) * (1.0 + sc) + sh


def _ada_kernel(c_ref, w_ref, b_ref, o_ref):
    c = c_ref[...]
    cond = c * jax.nn.sigmoid(c)
    o_ref[...] = jnp.sum(cond * w_ref[...], axis=0, keepdims=True) + b_ref[...]


def _ada_modulation(c, w_ada, b_ada):
    depth = w_ada.shape[0]
    nblk = w_ada.shape[2] // D_MODEL
    out = pl.pallas_call(
        _ada_kernel,
        grid=(depth, nblk),
        in_specs=[
            pl.BlockSpec((D_MODEL, 1), lambda l, j: (0, 0)),
            pl.BlockSpec((None, D_MODEL, D_MODEL), lambda l, j: (l, 0, j)),
            pl.BlockSpec((None, 1, D_MODEL), lambda l, j: (l, 0, j)),
        ],
        out_specs=pl.BlockSpec((None, 1, D_MODEL), lambda l, j: (l, 0, j)),
        out_shape=jax.ShapeDtypeStruct((depth, 1, w_ada.shape[2]), jnp.float32),
        name="ada_modulation",
    )(c.reshape(D_MODEL, 1), w_ada, b_ada[:, None, :])
    return out[:, 0, :]


def _proj_kernel(x_ref, g_ref, sc_ref, sh_ref, w_ref, wg_ref, qkg_ref, bd_ref, cw_ref, cb_ref,
                 o_ref, gate_ref, qzt_ref, vt_ref, hist_scr):
    tm = x_ref.shape[0]
    dh = A_HEAD_DIM

    @pl.when(pl.program_id(0) == 0)
    def _():
        hist_scr[:, :8, :] = jnp.zeros((2, 8, D_MODEL), jnp.float32)

    h = _modulated_norm(x_ref[...], g_ref[...], sc_ref[...], sh_ref[...])
    hb = h.astype(jnp.bfloat16)
    gate_ref[...] = _f32dot(hb, wg_ref[...])
    bd = bd_ref[...]
    lane = lax.broadcasted_iota(jnp.int32, (tm, LANES), 1)
    for j in range(w_ref.shape[1] // D_MODEL):
        y = _f32dot(hb, w_ref[:, j * D_MODEL:(j + 1) * D_MODEL])
        if j == 2:
            for hd in range(A_HEADS):
                vt_ref[hd, 0] = y[:, hd * LANES:(hd + 1) * LANES].T.astype(vt_ref.dtype)
        elif j in (3, 4):
            which = j - 3
            cols = slice(which * D_MODEL, (which + 1) * D_MODEL)
            hist = hist_scr.at[which]
            hist[8:, :] = y
            z = cb_ref[:, cols] + cw_ref[CONV_WIDTH - 1:CONV_WIDTH, cols] * y
            for back in range(1, CONV_WIDTH):
                z = z + cw_ref[CONV_WIDTH - 1 - back:CONV_WIDTH - back, cols] * hist[8 - back:8 - back + tm, :]
            hist[:8, :] = y[tm - 8:]
            half = 0.5 * z
            act = half + half * jnp.tanh(half)
            if which == 1:
                act = act * (M_HEAD_DIM ** -0.5)
            o_ref[:, (j - 2) * D_MODEL:(j - 1) * D_MODEL] = act.astype(o_ref.dtype)
        elif j > 4:
            o_ref[:, (j - 2) * D_MODEL:(j - 1) * D_MODEL] = y.astype(o_ref.dtype)
        else:
            for t in range(D_MODEL // 256):
                yt = y[:, t * 256:(t + 1) * 256]
                ss = _f32dot((yt * yt).astype(jnp.bfloat16), bd)
                yn = yt * lax.rsqrt(ss * (1.0 / A_HEAD_DIM) + EPS) * qkg_ref[j, :, t * 256:(t + 1) * 256]
                if j == 1:
                    o_ref[:, t * 256:(t + 1) * 256] = yn.astype(o_ref.dtype)
                    continue
                for hh in range(256 // LANES):
                    hd = t * (256 // LANES) + hh
                    qh = yn[:, hh * LANES:(hh + 1) * LANES]
                    qzt_ref[hd, 0, :, :tm] = jnp.where(lane < dh, qh, 0.0).T.astype(qzt_ref.dtype)
                    qzt_ref[hd, 0, :, tm:] = jnp.where(lane >= dh, qh, 0.0).T.astype(qzt_ref.dtype)


def _in_projection(x, g, sc, sh, w_main, w_gate, qk_gain, bd, conv_w, conv_b):
    s = x.shape[0]
    tm = min(ATT_BLOCK, s)
    nt = s // tm
    const = lambda i: (0, 0)
    vec = pl.BlockSpec((1, D_MODEL), const)
    resident = pl.Buffered(1)
    return pl.pallas_call(
        _proj_kernel,
        grid=(nt,),
        in_specs=[
            pl.BlockSpec((tm, D_MODEL), lambda i: (i, 0)),
            vec, vec, vec,
            pl.BlockSpec((D_MODEL, w_main.shape[1]), const, pipeline_mode=resident),
            pl.BlockSpec((D_MODEL, GATE_PAD), const, pipeline_mode=resident),
            pl.BlockSpec((2, 1, D_MODEL), lambda i: (0, 0, 0)),
            pl.BlockSpec((256, 256), const),
            pl.BlockSpec((CONV_WIDTH, 2 * D_MODEL), const),
            pl.BlockSpec((1, 2 * D_MODEL), const),
        ],
        out_specs=[
            pl.BlockSpec((tm, N_ROWMAJOR), lambda i: (i, 0)),
            pl.BlockSpec((tm, GATE_PAD), lambda i: (i, 0)),
            pl.BlockSpec((A_HEADS, 1, LANES, 2 * tm), lambda i: (0, i, 0, 0)),
            pl.BlockSpec((A_HEADS, 1, LANES, tm), lambda i: (0, i, 0, 0)),
        ],
        out_shape=[
            jax.ShapeDtypeStruct((s, N_ROWMAJOR), jnp.bfloat16),
            jax.ShapeDtypeStruct((s, GATE_PAD), jnp.float32),
            jax.ShapeDtypeStruct((A_HEADS, nt, LANES, 2 * tm), jnp.bfloat16),
            jax.ShapeDtypeStruct((A_HEADS, nt, LANES, tm), jnp.bfloat16),
        ],
        scratch_shapes=[pltpu.VMEM((2, tm + 8, D_MODEL), jnp.float32)],
        compiler_params=pltpu.CompilerParams(
            dimension_semantics=("arbitrary",), vmem_limit_bytes=VMEM_LIMIT),
        name="in_projection",
    )(x, g, sc, sh, w_main, w_gate, qk_gain, bd, conv_w, conv_b)


def _attn_kernel(lam_init, qzt_ref, k_ref, vt_ref, ones_ref, bias_ref, lamqk_ref, g_ref, o_ref,
                 sa_scr, sb_scr, mxa_scr, mxb_scr, m_scr, acc_scr):
    i = pl.program_id(1)
    bq = qzt_ref.shape[1] // 2
    bk = bq
    dv = 2 * A_HEAD_DIM

    m_scr[...] = jnp.full(m_scr.shape, NEG_BIG, jnp.float32)
    acc_scr[...] = jnp.zeros(acc_scr.shape, jnp.float32)

    def logits(j, s_ref, mx_ref):
        kb = k_ref[pl.ds(pl.multiple_of(j * bk, bk), bk), :]
        st = _f32dot(kb, qzt_ref[...])
        s_ref[...] = st
        mx_ref[...] = jnp.max(st, axis=0, keepdims=True)

    def consume(j, s_ref, mx_ref, bias_idx):
        st = s_ref[...]
        if bias_idx is None:
            mx = mx_ref[...]
        else:
            tile = bias_ref[bias_idx]
            st = st + jnp.concatenate([tile, tile], axis=1)
            mx = jnp.max(st, axis=0, keepdims=True)
        m_old = m_scr[...]
        m_new = jnp.maximum(m_old, mx)
        pt = jnp.exp2(st - m_new).astype(jnp.bfloat16)
        vt = jnp.concatenate([vt_ref[j], ones_ref[...]], axis=0)
        pv = _f32dot(vt, pt)
        acc_scr[...] = acc_scr[...] * jnp.exp2(m_old - m_new) + pv
        m_scr[...] = m_new

    NEAR, DIAG = 1, 0
    n_far = jnp.maximum(i - 1, 0)
    buf_a = (sa_scr, mxa_scr)
    buf_b = (sb_scr, mxb_scr)
    logits(0, *buf_a)

    def far_pair(j):
        logits(j + 1, *buf_b)
        consume(j, *buf_a, None)
        logits(j + 2, *buf_a)
        consume(j + 1, *buf_b, None)

    def far_octet(t, carry):
        for pair in range(4):
            far_pair(8 * t + 2 * pair)
        return carry

    lax.fori_loop(0, n_far // 8, far_octet, 0)

    @pl.when(n_far % 8 >= 4)
    def _():
        far_pair(8 * (n_far // 8))
        far_pair(8 * (n_far // 8) + 2)

    @pl.when(n_far % 4 >= 2)
    def _():
        far_pair(4 * (n_far // 4))

    @pl.when(i == 0)
    def _():
        consume(i, *buf_a, DIAG)

    @pl.when(jnp.logical_and(i > 0, n_far % 2 == 0))
    def _():
        logits(i, *buf_b)
        consume(i - 1, *buf_a, NEAR)
        consume(i, *buf_b, DIAG)

    @pl.when(n_far % 2 == 1)
    def _():
        logits(i - 1, *buf_b)
        consume(i - 2, *buf_a, None)
        logits(i, *buf_a)
        consume(i - 1, *buf_b, NEAR)
        consume(i, *buf_a, DIAG)

    lq = lamqk_ref[...]
    e1 = jnp.exp(jnp.sum(lq[0:1] * lq[1:2], axis=1, keepdims=True))
    e2 = jnp.exp(jnp.sum(lq[2:3] * lq[3:4], axis=1, keepdims=True))
    lam = e1 - e2 + lam_init

    o = acc_scr[:dv, :] * (1.0 / acc_scr[dv:dv + 1, :])
    ot = o[:, :bq] - lam * o[:, bq:]
    ms = jnp.mean(ot * ot, axis=0, keepdims=True)
    on = ot * lax.rsqrt(ms + EPS) * g_ref[...] * (1.0 - lam_init)
    o_ref[...] = on.T.astype(o_ref.dtype)


def _diff_attention(qzt, proj, vt, bias_tiles, lam_qk, subln_col, lam_init):
    s = proj.shape[0]
    bq = min(ATT_BLOCK, s)
    nq = s // bq
    pad_rows = 16
    ones_rows = jnp.zeros((pad_rows, bq), jnp.bfloat16).at[0].set(1.0)
    return pl.pallas_call(
        functools.partial(_attn_kernel, lam_init),
        grid=(A_HEADS, nq),
        in_specs=[
            pl.BlockSpec((None, None, LANES, 2 * bq), lambda h, i: (h, i, 0, 0)),
            pl.BlockSpec((s, LANES), lambda h, i: (0, h)),
            pl.BlockSpec((None, nq, LANES, bq), lambda h, i: (h, 0, 0, 0)),
            pl.BlockSpec((pad_rows, bq), lambda h, i: (0, 0)),
            pl.BlockSpec((None, 2, bq, bq), lambda h, i: (h, 0, 0, 0)),
            pl.BlockSpec((4, A_HEAD_DIM), lambda h, i: (0, 0)),
            pl.BlockSpec((LANES, 1), lambda h, i: (0, 0)),
        ],
        out_specs=pl.BlockSpec((bq, LANES), lambda h, i: (i, h)),
        out_shape=jax.ShapeDtypeStruct((s, A_WIDTH), jnp.bfloat16),
        scratch_shapes=[
            pltpu.VMEM((bq, 2 * bq), jnp.float32),
            pltpu.VMEM((bq, 2 * bq), jnp.float32),
            pltpu.VMEM((1, 2 * bq), jnp.float32),
            pltpu.VMEM((1, 2 * bq), jnp.float32),
            pltpu.VMEM((1, 2 * bq), jnp.float32),
            pltpu.VMEM((LANES + pad_rows, 2 * bq), jnp.float32),
        ],
        compiler_params=pltpu.CompilerParams(
            dimension_semantics=("arbitrary", "arbitrary"), vmem_limit_bytes=VMEM_LIMIT),
        name="diff_attention",
    )(qzt, proj, vt, ones_rows, bias_tiles, lam_qk, subln_col)


def _t5_bucket(dist):
    max_exact = NUM_BUCKETS // 2
    nf = jnp.maximum(dist, 1).astype(jnp.float32)
    large = max_exact + (jnp.log(nf / max_exact) / math.log(MAX_DISTANCE / max_exact)
                         * (NUM_BUCKETS - max_exact)).astype(jnp.int32)
    large = jnp.minimum(large, NUM_BUCKETS - 1)
    return jnp.where(dist < max_exact, dist, large)


def _attention_bias_tiles(rel_table, s):
    bq = min(ATT_BLOCK, s)
    dists = jnp.arange(2 * bq)
    bias = rel_table[_t5_bucket(dists)].T.astype(jnp.float32)
    far = rel_table[_t5_bucket(jnp.array([s - 1]))].T.astype(jnp.float32)
    bias = (bias - far) * LOG2E
    period = 2 * bq + 1
    masked = jnp.full((A_HEADS, bq + 1), NEG_BIG, jnp.float32)
    u_diag = jnp.concatenate([bias[:, :bq], masked], axis=1)
    u_near = jnp.concatenate([bias[:, bq:], jnp.zeros((A_HEADS, 1), jnp.float32), bias[:, :bq]], axis=1)
    u = jnp.stack([u_diag, u_near], axis=1)
    flat = jnp.tile(u, (1, 1, bq))[:, :, :bq * 2 * bq]
    return flat.reshape(A_HEADS, 2, bq, 2 * bq)[..., :bq]


def _mlstm_head(hd, c, q_ref, k_ref, v_ref, o_ref, gate_ref, gbias_ref, hg_ref, y_ref, state_scr, m_scr):
    L = q_ref.shape[0]
    dh = M_HEAD_DIM

    @pl.when(c == 0)
    def _():
        for scr in (state_scr, m_scr):
            scr[...] = jnp.zeros(scr.shape, jnp.float32)

    qb = q_ref[...]
    kb = k_ref[...]
    kc = kb.astype(jnp.float32)

    gb = gbias_ref[...]
    gates = gate_ref[...] + gb
    sub = lax.broadcasted_iota(jnp.int32, gates.shape, 0)
    ig_row = jnp.sum(jnp.where(sub == hd, gates, 0.0), axis=0, keepdims=True)
    fg_row = jnp.sum(jnp.where(sub == hd + M_HEADS, gates, 0.0), axis=0, keepdims=True)
    lf_row = jax.nn.log_sigmoid(fg_row)

    t_idx = lax.broadcasted_iota(jnp.int32, (L, L), 0)
    s_idx = lax.broadcasted_iota(jnp.int32, (L, L), 1)
    tril = s_idx <= t_idx
    eye = s_idx == t_idx

    def to_col(r):
        return jnp.sum(jnp.where(eye, r, 0.0), axis=1, keepdims=True)

    def to_row(col):
        return jnp.sum(jnp.where(eye, col, 0.0), axis=0, keepdims=True)

    bcum_col = jnp.sum(jnp.where(tril, lf_row, 0.0), axis=1, keepdims=True)
    bcum_row = to_row(bcum_col)
    ig_col = to_col(ig_row)
    b_last = bcum_col[L - 1:L, :]
    m_prev = m_scr[0:1, 0:1]

    dmat = jnp.where(tril, bcum_col - bcum_row + ig_row, NEG_BIG)
    inter = bcum_col + m_prev
    m_t = jnp.maximum(jnp.max(dmat, axis=1, keepdims=True), inter)
    qk = lax.dot_general(qb, kb, (((1,), (1,)), ((), ())), preferred_element_type=jnp.float32)
    scores = qk * jnp.exp(dmat - m_t)
    a_inter = jnp.exp(inter - m_t)

    lane_w = lax.broadcasted_iota(jnp.int32, (L, LANES), 1)
    v_ext = jnp.concatenate(
        [v_ref[...], jnp.where(lane_w == 0, 1.0, 0.0).astype(jnp.bfloat16)], axis=1)
    state = state_scr[...]
    res = _f32dot(scores.astype(jnp.bfloat16), v_ext) + a_inter * _f32dot(qb, state.astype(jnp.bfloat16))
    num = res[:, :dh]
    den = res[:, dh:dh + 1]
    h = num / jnp.maximum(jnp.abs(den), jnp.exp(-m_t))

    g_col = b_last - bcum_col + ig_col
    m_new = jnp.maximum(b_last + m_prev, jnp.max(g_col, axis=0, keepdims=True))
    wk = jnp.exp(g_col - m_new)
    decay = jnp.exp(b_last + m_prev - m_new)
    kw = (kc * wk).astype(jnp.bfloat16)
    upd = lax.dot_general(kw, v_ext, (((0,), (0,)), ((), ())), preferred_element_type=jnp.float32)
    state_scr[...] = decay * state + upd
    m_scr[...] = jnp.broadcast_to(m_new, m_scr.shape)

    hn = h * lax.rsqrt(jnp.mean(h * h, axis=-1, keepdims=True) + EPS) * hg_ref[...]
    o_gate = 0.5 + 0.5 * jnp.tanh(0.5 * o_ref[...].astype(jnp.float32))
    y_ref[...] = (o_gate * hn).astype(y_ref.dtype)


def _mlstm_kernel(q_ref, k_ref, v_ref, o_ref, gate_ref, gbias_ref, hg_ref, y_ref, state_scr, m_scr):
    dh = M_HEAD_DIM
    for hh in range(M_HEADS_PER_STEP):
        cols = pl.ds(hh * dh, dh)
        _mlstm_head(pl.program_id(0) * M_HEADS_PER_STEP + hh, pl.program_id(1),
                    q_ref.at[:, cols], k_ref.at[:, cols], v_ref.at[:, cols], o_ref.at[:, cols],
                    gate_ref, gbias_ref, hg_ref, y_ref.at[:, cols], state_scr.at[hh], m_scr.at[hh])


def _mlstm(proj, gates_t, gate_bias, mhn_g):
    s = proj.shape[0]
    L = min(M_CHUNK, s)
    nc = s // L
    dh = M_HEAD_DIM
    hps = M_HEADS_PER_STEP
    w = hps * dh
    ng = M_HEADS // hps
    base = D_MODEL // w
    return pl.pallas_call(
        _mlstm_kernel,
        grid=(ng, nc),
        in_specs=[
            pl.BlockSpec((L, w), lambda g, c: (c, base + g)),
            pl.BlockSpec((L, w), lambda g, c: (c, base + ng + g)),
            pl.BlockSpec((L, w), lambda g, c: (c, base + 2 * ng + g)),
            pl.BlockSpec((L, w), lambda g, c: (c, base + 3 * ng + g)),
            pl.BlockSpec((2 * M_HEADS, L), lambda g, c: (0, c)),
            pl.BlockSpec((2 * M_HEADS, 1), lambda g, c: (0, 0)),
            pl.BlockSpec((1, dh), lambda g, c: (0, 0)),
        ],
        out_specs=pl.BlockSpec((L, w), lambda g, c: (c, g)),
        out_shape=jax.ShapeDtypeStruct((s, D_MODEL), jnp.bfloat16),
        scratch_shapes=[
            pltpu.VMEM((hps, dh, STATE_W), jnp.float32),
            pltpu.VMEM((hps, 8, LANES), jnp.float32),
        ],
        compiler_params=pltpu.CompilerParams(
            dimension_semantics=("arbitrary", "arbitrary"), vmem_limit_bytes=VMEM_LIMIT),
        name="mlstm",
    )(proj, proj, proj, proj, gates_t, gate_bias, mhn_g)


def _merge_kernel(x_ref, ya_ref, ym_ref, ga_ref, gm_ref, wa_ref, wm_ref, wo_ref, gt_ref, o_ref):
    a = _f32dot(ya_ref[...], wa_ref[...])
    b = _f32dot(ym_ref[...], wm_ref[...])
    merged = (jax.nn.sigmoid(ga_ref[...].astype(jnp.float32)) * a
              + jax.nn.sigmoid(gm_ref[...].astype(jnp.float32)) * b)
    o_ref[...] = x_ref[...] + gt_ref[...] * _f32dot(merged.astype(jnp.bfloat16), wo_ref[...])


def _merge(x, ya, ym, proj, w_a, w_m, w_out, gt):
    s = x.shape[0]
    tm = min(MLP_TILE, s)
    row = lambda i: (i, 0)
    const = lambda i: (0, 0)
    wspec = pl.BlockSpec((D_MODEL, D_MODEL), const, pipeline_mode=pl.Buffered(1))
    return pl.pallas_call(
        _merge_kernel,
        grid=(s // tm,),
        in_specs=[
            pl.BlockSpec((tm, D_MODEL), row),
            pl.BlockSpec((tm, D_MODEL), row),
            pl.BlockSpec((tm, D_MODEL), row),
            pl.BlockSpec((tm, D_MODEL), lambda i: (i, 5)),
            pl.BlockSpec((tm, D_MODEL), lambda i: (i, 6)),
            wspec, wspec, wspec,
            pl.BlockSpec((1, D_MODEL), const),
        ],
        out_specs=pl.BlockSpec((tm, D_MODEL), row),
        out_shape=jax.ShapeDtypeStruct((s, D_MODEL), jnp.float32),
        compiler_params=pltpu.CompilerParams(
            dimension_semantics=("arbitrary",), vmem_limit_bytes=VMEM_LIMIT),
        name="merge_out_projection",
    )(x, ya, ym, proj, proj, w_a, w_m, w_out, gt)


def _mlp_kernel(x_ref, g_ref, sc_ref, sh_ref, gt_ref, w1_ref, w2_ref, o_ref):
    x = x_ref[...]
    hb = _modulated_norm(x, g_ref[...], sc_ref[...], sh_ref[...]).astype(jnp.bfloat16)
    acc = None
    for t in range(D_FF // D_MODEL):
        u = jnp.maximum(_f32dot(hb, w1_ref[:, t * D_MODEL:(t + 1) * D_MODEL]), 0.0)
        part = _f32dot((u * u).astype(jnp.bfloat16), w2_ref[t * D_MODEL:(t + 1) * D_MODEL, :])
        acc = part if acc is None else acc + part
    o_ref[...] = x + gt_ref[...] * acc


def _mlp(x, g, sc, sh, gt, w1, w2):
    s = x.shape[0]
    tm = min(MLP_TILE, s)
    row = lambda i: (i, 0)
    const = lambda i: (0, 0)
    vec = pl.BlockSpec((1, D_MODEL), const)
    return pl.pallas_call(
        _mlp_kernel,
        grid=(s // tm,),
        in_specs=[
            pl.BlockSpec((tm, D_MODEL), row),
            vec, vec, vec, vec,
            pl.BlockSpec((D_MODEL, D_FF), const, pipeline_mode=pl.Buffered(1)),
            pl.BlockSpec((D_FF, D_MODEL), const, pipeline_mode=pl.Buffered(1)),
        ],
        out_specs=pl.BlockSpec((tm, D_MODEL), row),
        out_shape=jax.ShapeDtypeStruct((s, D_MODEL), jnp.float32),
        compiler_params=pltpu.CompilerParams(
            dimension_semantics=("arbitrary",), vmem_limit_bytes=VMEM_LIMIT),
        name="relu2_mlp",
    )(x, g, sc, sh, gt, w1, w2)


def kernel(x, c, w_ada, b_ada, norm_mix_g, norm_ffn_g, w_in, b_igate, b_fgate, qn_g, kn_g, lam_qk,
           subln_g, rel_table, conv_w, conv_b, mhn_g, w_a, w_m, w_out, w_ff1, w_ff2):
    b, s, d = x.shape
    assert b == 1 and d == D_MODEL
    depth = w_ada.shape[0]
    bf = jnp.bfloat16
    xs = x[0]

    mod = _ada_modulation(c, w_ada, b_ada)
    bias_tiles = _attention_bias_tiles(rel_table, s)
    blk = A_HEAD_DIM
    bd = (jnp.arange(256)[:, None] // blk == jnp.arange(256)[None, :] // blk).astype(bf)

    gate_lo = 7 * D_MODEL
    gate_hi = gate_lo + 2 * M_HEADS
    for l in range(depth):
        lam_init = 0.8 - 0.6 * math.exp(-0.3 * l)
        sh_a, sc_a, gt_a, sh_f, sc_f, gt_f = [mod[l:l + 1, t * d:(t + 1) * d] for t in range(6)]

        w = w_in[l]
        w_main = jnp.concatenate([w[:, :gate_lo], w[:, gate_hi:]], axis=1).astype(bf)
        w_gate = jnp.pad(w[:, gate_lo:gate_hi], ((0, 0), (0, GATE_PAD - 2 * M_HEADS))).astype(bf)
        reps = A_WIDTH // A_HEAD_DIM
        qk_gain = jnp.stack([jnp.tile(qn_g[l], reps) * (A_HEAD_DIM ** -0.5 * LOG2E),
                             jnp.tile(kn_g[l], reps)])[:, None, :]

        proj, gates, qzt, vt = _in_projection(xs, norm_mix_g[l][None], sc_a, sh_a, w_main, w_gate,
                                              qk_gain, bd, conv_w[l], conv_b[l][None])

        ya = _diff_attention(qzt, proj, vt, bias_tiles, lam_qk[l], subln_g[l][:, None], lam_init)

        gates_t = gates[:, :2 * M_HEADS].T
        gate_bias = jnp.concatenate([b_igate[l], b_fgate[l]])[:, None]
        ym = _mlstm(proj, gates_t, gate_bias, mhn_g[l][None])

        xs = _merge(xs, ya, ym, proj, w_a[l].astype(bf), w_m[l].astype(bf), w_out[l].astype(bf), gt_a)
        xs = _mlp(xs, norm_ffn_g[l][None], sc_f, sh_f, gt_f, w_ff1[l].astype(bf), w_ff2[l].astype(bf))
    return xs[None]
```

```python
import functools
import math

import jax
import jax.numpy as jnp
from jax import lax
from jax.experimental import pallas as pl
from jax.experimental.pallas import tpu as pltpu

D_MODEL = 1024
A_HEADS = 8
A_HEAD_DIM = 64
A_WIDTH = A_HEADS * 2 * A_HEAD_DIM
M_HEADS = 4
M_HEAD_DIM = D_MODEL // M_HEADS
CONV_WIDTH = 4
NUM_BUCKETS = 32
MAX_DISTANCE = 128
D_FF = 4 * D_MODEL
EPS = 1e-6

LANES = 128
GATE_PAD = LANES
N_ROWMAJOR = 7 * D_MODEL
VMEM_LIMIT = 56 * 1024 * 1024
NEG_BIG = -1e30
LOG2E = math.log2(math.e)

ATT_BLOCK = 512
M_CHUNK = 256
M_HEADS_PER_STEP = 2
MLP_TILE = 512
STATE_W = M_HEAD_DIM + LANES


def _f32dot(a, b):
    return jnp.dot(a, b, preferred_element_type=jnp.float32)


def _modulated_norm(x, g, sc, sh):
    y = x * lax.rsqrt(jnp.mean(x * x, axis=-1, keepdims=True) + EPS)
    return (y * g) * (1.0 + sc) + sh


def _ada_kernel(c_ref, w_ref, b_ref, o_ref):
    c = c_ref[...]
    cond = c * jax.nn.sigmoid(c)
    o_ref[...] = jnp.sum(cond * w_ref[...], axis=0, keepdims=True) + b_ref[...]


def _ada_modulation(c, w_ada, b_ada):
    depth = w_ada.shape[0]
    nblk = w_ada.shape[2] // D_MODEL
    out = pl.pallas_call(
        _ada_kernel,
        grid=(depth, nblk),
        in_specs=[
            pl.BlockSpec((D_MODEL, 1), lambda l, j: (0, 0)),
            pl.BlockSpec((None, D_MODEL, D_MODEL), lambda l, j: (l, 0, j)),
            pl.BlockSpec((None, 1, D_MODEL), lambda l, j: (l, 0, j)),
        ],
        out_specs=pl.BlockSpec((None, 1, D_MODEL), lambda l, j: (l, 0, j)),
        out_shape=jax.ShapeDtypeStruct((depth, 1, w_ada.shape[2]), jnp.float32),
        name="ada_modulation",
    )(c.reshape(D_MODEL, 1), w_ada, b_ada[:, None, :])
    return out[:, 0, :]


def _proj_kernel(x_ref, g_ref, sc_ref, sh_ref, w_ref, wg_ref, qkg_ref, bd_ref,
                 o_ref, gate_ref, qzt_ref, vt_ref):
    tm = x_ref.shape[0]
    dh = A_HEAD_DIM
    h = _modulated_norm(x_ref[...], g_ref[...], sc_ref[...], sh_ref[...])
    hb = h.astype(jnp.bfloat16)
    gate_ref[...] = _f32dot(hb, wg_ref[...])
    bd = bd_ref[...]
    lane = lax.broadcasted_iota(jnp.int32, (tm, LANES), 1)
    for j in range(w_ref.shape[1] // D_MODEL):
        y = _f32dot(hb, w_ref[:, j * D_MODEL:(j + 1) * D_MODEL])
        if j == 2:
            for hd in range(A_HEADS):
                vt_ref[hd, 0] = y[:, hd * LANES:(hd + 1) * LANES].T.astype(vt_ref.dtype)
        elif j > 2:
            o_ref[:, (j - 2) * D_MODEL:(j - 1) * D_MODEL] = y.astype(o_ref.dtype)
        else:
            for t in range(D_MODEL // 256):
                yt = y[:, t * 256:(t + 1) * 256]
                ss = _f32dot((yt * yt).astype(jnp.bfloat16), bd)
                yn = yt * lax.rsqrt(ss * (1.0 / A_HEAD_DIM) + EPS) * qkg_ref[j, :, t * 256:(t + 1) * 256]
                if j == 1:
                    o_ref[:, t * 256:(t + 1) * 256] = yn.astype(o_ref.dtype)
                    continue
                for hh in range(256 // LANES):
                    hd = t * (256 // LANES) + hh
                    qh = yn[:, hh * LANES:(hh + 1) * LANES]
                    qzt_ref[hd, 0, :, :tm] = jnp.where(lane < dh, qh, 0.0).T.astype(qzt_ref.dtype)
                    qzt_ref[hd, 0, :, tm:] = jnp.where(lane >= dh, qh, 0.0).T.astype(qzt_ref.dtype)


def _in_projection(x, g, sc, sh, w_main, w_gate, qk_gain, bd):
    s = x.shape[0]
    tm = min(ATT_BLOCK, s)
    nt = s // tm
    const = lambda i: (0, 0)
    vec = pl.BlockSpec((1, D_MODEL), const)
    resident = pl.Buffered(1)
    return pl.pallas_call(
        _proj_kernel,
        grid=(nt,),
        in_specs=[
            pl.BlockSpec((tm, D_MODEL), lambda i: (i, 0)),
            vec, vec, vec,
            pl.BlockSpec((D_MODEL, w_main.shape[1]), const, pipeline_mode=resident),
            pl.BlockSpec((D_MODEL, GATE_PAD), const, pipeline_mode=resident),
            pl.BlockSpec((2, 1, D_MODEL), lambda i: (0, 0, 0)),
            pl.BlockSpec((256, 256), const),
        ],
        out_specs=[
            pl.BlockSpec((tm, N_ROWMAJOR), lambda i: (i, 0)),
            pl.BlockSpec((tm, GATE_PAD), lambda i: (i, 0)),
            pl.BlockSpec((A_HEADS, 1, LANES, 2 * tm), lambda i: (0, i, 0, 0)),
            pl.BlockSpec((A_HEADS, 1, LANES, tm), lambda i: (0, i, 0, 0)),
        ],
        out_shape=[
            jax.ShapeDtypeStruct((s, N_ROWMAJOR), jnp.bfloat16),
            jax.ShapeDtypeStruct((s, GATE_PAD), jnp.float32),
            jax.ShapeDtypeStruct((A_HEADS, nt, LANES, 2 * tm), jnp.bfloat16),
            jax.ShapeDtypeStruct((A_HEADS, nt, LANES, tm), jnp.bfloat16),
        ],
        compiler_params=pltpu.CompilerParams(
            dimension_semantics=("arbitrary",), vmem_limit_bytes=VMEM_LIMIT),
        name="in_projection",
    )(x, g, sc, sh, w_main, w_gate, qk_gain, bd)


def _attn_kernel(lam_init, qzt_ref, k_ref, vt_ref, ones_ref, bias_ref, lamqk_ref, g_ref, o_ref,
                 sa_scr, sb_scr, mxa_scr, mxb_scr, m_scr, acc_scr):
    i = pl.program_id(1)
    bq = qzt_ref.shape[1] // 2
    bk = bq
    dv = 2 * A_HEAD_DIM

    m_scr[...] = jnp.full(m_scr.shape, NEG_BIG, jnp.float32)
    acc_scr[...] = jnp.zeros(acc_scr.shape, jnp.float32)

    def logits(j, s_ref, mx_ref):
        kb = k_ref[pl.ds(pl.multiple_of(j * bk, bk), bk), :]
        st = _f32dot(kb, qzt_ref[...])
        s_ref[...] = st
        mx_ref[...] = jnp.max(st, axis=0, keepdims=True)

    def consume(j, s_ref, mx_ref, bias_idx):
        st = s_ref[...]
        if bias_idx is None:
            mx = mx_ref[...]
        else:
            tile = bias_ref[bias_idx]
            st = st + jnp.concatenate([tile, tile], axis=1)
            mx = jnp.max(st, axis=0, keepdims=True)
        m_old = m_scr[...]
        m_new = jnp.maximum(m_old, mx)
        pt = jnp.exp2(st - m_new).astype(jnp.bfloat16)
        vt = jnp.concatenate([vt_ref[j], ones_ref[...]], axis=0)
        pv = _f32dot(vt, pt)
        acc_scr[...] = acc_scr[...] * jnp.exp2(m_old - m_new) + pv
        m_scr[...] = m_new

    NEAR, DIAG = 1, 0
    n_far = jnp.maximum(i - 1, 0)
    buf_a = (sa_scr, mxa_scr)
    buf_b = (sb_scr, mxb_scr)
    logits(0, *buf_a)

    def far_pair(j):
        logits(j + 1, *buf_b)
        consume(j, *buf_a, None)
        logits(j + 2, *buf_a)
        consume(j + 1, *buf_b, None)

    def far_octet(t, carry):
        for pair in range(4):
            far_pair(8 * t + 2 * pair)
        return carry

    lax.fori_loop(0, n_far // 8, far_octet, 0)

    @pl.when(n_far % 8 >= 4)
    def _():
        far_pair(8 * (n_far // 8))
        far_pair(8 * (n_far // 8) + 2)

    @pl.when(n_far % 4 >= 2)
    def _():
        far_pair(4 * (n_far // 4))

    @pl.when(i == 0)
    def _():
        consume(i, *buf_a, DIAG)

    @pl.when(jnp.logical_and(i > 0, n_far % 2 == 0))
    def _():
        logits(i, *buf_b)
        consume(i - 1, *buf_a, NEAR)
        consume(i, *buf_b, DIAG)

    @pl.when(n_far % 2 == 1)
    def _():
        logits(i - 1, *buf_b)
        consume(i - 2, *buf_a, None)
        logits(i, *buf_a)
        consume(i - 1, *buf_b, NEAR)
        consume(i, *buf_a, DIAG)

    lq = lamqk_ref[...]
    e1 = jnp.exp(jnp.sum(lq[0:1] * lq[1:2], axis=1, keepdims=True))
    e2 = jnp.exp(jnp.sum(lq[2:3] * lq[3:4], axis=1, keepdims=True))
    lam = e1 - e2 + lam_init

    o = acc_scr[:dv, :] * (1.0 / acc_scr[dv:dv + 1, :])
    ot = o[:, :bq] - lam * o[:, bq:]
    ms = jnp.mean(ot * ot, axis=0, keepdims=True)
    on = ot * lax.rsqrt(ms + EPS) * g_ref[...] * (1.0 - lam_init)
    o_ref[...] = on.T.astype(o_ref.dtype)


def _diff_attention(qzt, proj, vt, bias_tiles, lam_qk, subln_col, lam_init):
    s = proj.shape[0]
    bq = min(ATT_BLOCK, s)
    nq = s // bq
    pad_rows = 16
    ones_rows = jnp.zeros((pad_rows, bq), jnp.bfloat16).at[0].set(1.0)
    return pl.pallas_call(
        functools.partial(_attn_kernel, lam_init),
        grid=(A_HEADS, nq),
        in_specs=[
            pl.BlockSpec((None, None, LANES, 2 * bq), lambda h, i: (h, i, 0, 0)),
            pl.BlockSpec((s, LANES), lambda h, i: (0, h)),
            pl.BlockSpec((None, nq, LANES, bq), lambda h, i: (h, 0, 0, 0)),
            pl.BlockSpec((pad_rows, bq), lambda h, i: (0, 0)),
            pl.BlockSpec((None, 2, bq, bq), lambda h, i: (h, 0, 0, 0)),
            pl.BlockSpec((4, A_HEAD_DIM), lambda h, i: (0, 0)),
            pl.BlockSpec((LANES, 1), lambda h, i: (0, 0)),
        ],
        out_specs=pl.BlockSpec((bq, LANES), lambda h, i: (i, h)),
        out_shape=jax.ShapeDtypeStruct((s, A_WIDTH), jnp.bfloat16),
        scratch_shapes=[
            pltpu.VMEM((bq, 2 * bq), jnp.float32),
            pltpu.VMEM((bq, 2 * bq), jnp.float32),
            pltpu.VMEM((1, 2 * bq), jnp.float32),
            pltpu.VMEM((1, 2 * bq), jnp.float32),
            pltpu.VMEM((1, 2 * bq), jnp.float32),
            pltpu.VMEM((LANES + pad_rows, 2 * bq), jnp.float32),
        ],
        compiler_params=pltpu.CompilerParams(
            dimension_semantics=("arbitrary", "arbitrary"), vmem_limit_bytes=VMEM_LIMIT),
        name="diff_attention",
    )(qzt, proj, vt, ones_rows, bias_tiles, lam_qk, subln_col)


def _t5_bucket(dist):
    max_exact = NUM_BUCKETS // 2
    nf = jnp.maximum(dist, 1).astype(jnp.float32)
    large = max_exact + (jnp.log(nf / max_exact) / math.log(MAX_DISTANCE / max_exact)
                         * (NUM_BUCKETS - max_exact)).astype(jnp.int32)
    large = jnp.minimum(large, NUM_BUCKETS - 1)
    return jnp.where(dist < max_exact, dist, large)


def _attention_bias_tiles(rel_table, s):
    bq = min(ATT_BLOCK, s)
    dists = jnp.arange(2 * bq)
    bias = rel_table[_t5_bucket(dists)].T.astype(jnp.float32)
    far = rel_table[_t5_bucket(jnp.array([s - 1]))].T.astype(jnp.float32)
    bias = (bias - far) * LOG2E
    period = 2 * bq + 1
    masked = jnp.full((A_HEADS, bq + 1), NEG_BIG, jnp.float32)
    u_diag = jnp.concatenate([bias[:, :bq], masked], axis=1)
    u_near = jnp.concatenate([bias[:, bq:], jnp.zeros((A_HEADS, 1), jnp.float32), bias[:, :bq]], axis=1)
    u = jnp.stack([u_diag, u_near], axis=1)
    flat = jnp.tile(u, (1, 1, bq))[:, :, :bq * 2 * bq]
    return flat.reshape(A_HEADS, 2, bq, 2 * bq)[..., :bq]


def _mlstm_head(hd, c, q_ref, k_ref, v_ref, o_ref, gate_ref, gbias_ref, cwq_ref, cwk_ref, cbq_ref, cbk_ref,
                shift_ref, hg_ref, y_ref, state_scr, m_scr, pq_scr, pk_scr):
    L = q_ref.shape[0]
    dh = M_HEAD_DIM

    @pl.when(c == 0)
    def _():
        for scr in (state_scr, m_scr, pq_scr, pk_scr):
            scr[...] = jnp.zeros(scr.shape, jnp.float32)

    row8 = lax.broadcasted_iota(jnp.int32, (8, dh), 0)

    def conv_silu(u_ref, tail_scr, w_ref, b_ref):
        ub = u_ref[...]
        u = ub.astype(jnp.float32)
        w = w_ref[...]
        taps = _f32dot(shift_ref[...], ub)
        tail = tail_scr[...]
        y = b_ref[...] + w[CONV_WIDTH - 1:CONV_WIDTH] * u
        top = jnp.zeros((8, dh), jnp.float32)
        for back in range(1, CONV_WIDTH):
            wb = w[CONV_WIDTH - 1 - back:CONV_WIDTH - back]
            y = y + wb * taps[(back - 1) * L:back * L]
            top = top + wb * jnp.where(row8 < back, pltpu.roll(tail, back, 0), 0.0)
        y = jnp.concatenate([y[:8] + top, y[8:]], axis=0)
        tail_scr[...] = u[L - 8:]
        half = 0.5 * y
        return half + half * jnp.tanh(half)

    qc = conv_silu(q_ref, pq_scr, cwq_ref, cbq_ref)
    kc = conv_silu(k_ref, pk_scr, cwk_ref, cbk_ref) * (dh ** -0.5)
    qb = qc.astype(jnp.bfloat16)
    kb = kc.astype(jnp.bfloat16)

    gb = gbias_ref[...]
    gates = gate_ref[...] + gb
    sub = lax.broadcasted_iota(jnp.int32, gates.shape, 0)
    ig_row = jnp.sum(jnp.where(sub == hd, gates, 0.0), axis=0, keepdims=True)
    fg_row = jnp.sum(jnp.where(sub == hd + M_HEADS, gates, 0.0), axis=0, keepdims=True)
    lf_row = jax.nn.log_sigmoid(fg_row)

    t_idx = lax.broadcasted_iota(jnp.int32, (L, L), 0)
    s_idx = lax.broadcasted_iota(jnp.int32, (L, L), 1)
    tril = s_idx <= t_idx
    eye = s_idx == t_idx

    def to_col(r):
        return jnp.sum(jnp.where(eye, r, 0.0), axis=1, keepdims=True)

    def to_row(col):
        return jnp.sum(jnp.where(eye, col, 0.0), axis=0, keepdims=True)

    bcum_col = jnp.sum(jnp.where(tril, lf_row, 0.0), axis=1, keepdims=True)
    bcum_row = to_row(bcum_col)
    ig_col = to_col(ig_row)
    b_last = bcum_col[L - 1:L, :]
    m_prev = m_scr[0:1, 0:1]

    dmat = jnp.where(tril, bcum_col - bcum_row + ig_row, NEG_BIG)
    inter = bcum_col + m_prev
    m_t = jnp.maximum(jnp.max(dmat, axis=1, keepdims=True), inter)
    qk = lax.dot_general(qb, kb, (((1,), (1,)), ((), ())), preferred_element_type=jnp.float32)
    scores = qk * jnp.exp(dmat - m_t)
    a_inter = jnp.exp(inter - m_t)

    lane_w = lax.broadcasted_iota(jnp.int32, (L, LANES), 1)
    v_ext = jnp.concatenate(
        [v_ref[...], jnp.where(lane_w == 0, 1.0, 0.0).astype(jnp.bfloat16)], axis=1)
    state = state_scr[...]
    res = _f32dot(scores.astype(jnp.bfloat16), v_ext) + a_inter * _f32dot(qb, state.astype(jnp.bfloat16))
    num = res[:, :dh]
    den = res[:, dh:dh + 1]
    h = num / jnp.maximum(jnp.abs(den), jnp.exp(-m_t))

    g_col = b_last - bcum_col + ig_col
    m_new = jnp.maximum(b_last + m_prev, jnp.max(g_col, axis=0, keepdims=True))
    wk = jnp.exp(g_col - m_new)
    decay = jnp.exp(b_last + m_prev - m_new)
    kw = (kc * wk).astype(jnp.bfloat16)
    upd = lax.dot_general(kw, v_ext, (((0,), (0,)), ((), ())), preferred_element_type=jnp.float32)
    state_scr[...] = decay * state + upd
    m_scr[...] = jnp.broadcast_to(m_new, m_scr.shape)

    hn = h * lax.rsqrt(jnp.mean(h * h, axis=-1, keepdims=True) + EPS) * hg_ref[...]
    o_gate = 0.5 + 0.5 * jnp.tanh(0.5 * o_ref[...].astype(jnp.float32))
    y_ref[...] = (o_gate * hn).astype(y_ref.dtype)


def _mlstm_kernel(q_ref, k_ref, v_ref, o_ref, gate_ref, gbias_ref, cwq_ref, cwk_ref, cbq_ref, cbk_ref,
                  shift_ref, hg_ref, y_ref, state_scr, m_scr, pq_scr, pk_scr):
    dh = M_HEAD_DIM
    for hh in range(M_HEADS_PER_STEP):
        cols = pl.ds(hh * dh, dh)
        _mlstm_head(pl.program_id(0) * M_HEADS_PER_STEP + hh, pl.program_id(1),
                    q_ref.at[:, cols], k_ref.at[:, cols], v_ref.at[:, cols], o_ref.at[:, cols],
                    gate_ref, gbias_ref, cwq_ref.at[:, cols], cwk_ref.at[:, cols],
                    cbq_ref.at[:, cols], cbk_ref.at[:, cols], shift_ref, hg_ref, y_ref.at[:, cols],
                    state_scr.at[hh], m_scr.at[hh], pq_scr.at[hh], pk_scr.at[hh])


def _mlstm(proj, gates_t, gate_bias, conv_w, conv_b, mhn_g):
    s = proj.shape[0]
    L = min(M_CHUNK, s)
    nc = s // L
    dh = M_HEAD_DIM
    hps = M_HEADS_PER_STEP
    w = hps * dh
    ng = M_HEADS // hps
    base = D_MODEL // w
    taps = jnp.arange((CONV_WIDTH - 1) * L)
    shift = (jnp.arange(L)[None, :] == (taps % L - taps // L - 1)[:, None]).astype(jnp.bfloat16)
    return pl.pallas_call(
        _mlstm_kernel,
        grid=(ng, nc),
        in_specs=[
            pl.BlockSpec((L, w), lambda g, c: (c, base + g)),
            pl.BlockSpec((L, w), lambda g, c: (c, base + ng + g)),
            pl.BlockSpec((L, w), lambda g, c: (c, base + 2 * ng + g)),
            pl.BlockSpec((L, w), lambda g, c: (c, base + 3 * ng + g)),
            pl.BlockSpec((2 * M_HEADS, L), lambda g, c: (0, c)),
            pl.BlockSpec((2 * M_HEADS, 1), lambda g, c: (0, 0)),
            pl.BlockSpec((CONV_WIDTH, w), lambda g, c: (0, g)),
            pl.BlockSpec((CONV_WIDTH, w), lambda g, c: (0, ng + g)),
            pl.BlockSpec((1, w), lambda g, c: (0, g)),
            pl.BlockSpec((1, w), lambda g, c: (0, ng + g)),
            pl.BlockSpec(((CONV_WIDTH - 1) * L, L), lambda g, c: (0, 0)),
            pl.BlockSpec((1, dh), lambda g, c: (0, 0)),
        ],
        out_specs=pl.BlockSpec((L, w), lambda g, c: (c, g)),
        out_shape=jax.ShapeDtypeStruct((s, D_MODEL), jnp.bfloat16),
        scratch_shapes=[
            pltpu.VMEM((hps, dh, STATE_W), jnp.float32),
            pltpu.VMEM((hps, 8, LANES), jnp.float32),
            pltpu.VMEM((hps, 8, dh), jnp.float32),
            pltpu.VMEM((hps, 8, dh), jnp.float32),
        ],
        compiler_params=pltpu.CompilerParams(
            dimension_semantics=("arbitrary", "arbitrary"), vmem_limit_bytes=VMEM_LIMIT),
        name="mlstm",
    )(proj, proj, proj, proj, gates_t, gate_bias, conv_w, conv_w, conv_b, conv_b, shift, mhn_g)


def _merge_kernel(x_ref, ya_ref, ym_ref, ga_ref, gm_ref, wa_ref, wm_ref, wo_ref, gt_ref, o_ref):
    a = _f32dot(ya_ref[...], wa_ref[...])
    b = _f32dot(ym_ref[...], wm_ref[...])
    merged = (jax.nn.sigmoid(ga_ref[...].astype(jnp.float32)) * a
              + jax.nn.sigmoid(gm_ref[...].astype(jnp.float32)) * b)
    o_ref[...] = x_ref[...] + gt_ref[...] * _f32dot(merged.astype(jnp.bfloat16), wo_ref[...])


def _merge(x, ya, ym, proj, w_a, w_m, w_out, gt):
    s = x.shape[0]
    tm = min(MLP_TILE, s)
    row = lambda i: (i, 0)
    const = lambda i: (0, 0)
    wspec = pl.BlockSpec((D_MODEL, D_MODEL), const, pipeline_mode=pl.Buffered(1))
    return pl.pallas_call(
        _merge_kernel,
        grid=(s // tm,),
        in_specs=[
            pl.BlockSpec((tm, D_MODEL), row),
            pl.BlockSpec((tm, D_MODEL), row),
            pl.BlockSpec((tm, D_MODEL), row),
            pl.BlockSpec((tm, D_MODEL), lambda i: (i, 5)),
            pl.BlockSpec((tm, D_MODEL), lambda i: (i, 6)),
            wspec, wspec, wspec,
            pl.BlockSpec((1, D_MODEL), const),
        ],
        out_specs=pl.BlockSpec((tm, D_MODEL), row),
        out_shape=jax.ShapeDtypeStruct((s, D_MODEL), jnp.float32),
        compiler_params=pltpu.CompilerParams(
            dimension_semantics=("arbitrary",), vmem_limit_bytes=VMEM_LIMIT),
        name="merge_out_projection",
    )(x, ya, ym, proj, proj, w_a, w_m, w_out, gt)


def _mlp_kernel(x_ref, g_ref, sc_ref, sh_ref, gt_ref, w1_ref, w2_ref, o_ref):
    x = x_ref[...]
    hb = _modulated_norm(x, g_ref[...], sc_ref[...], sh_ref[...]).astype(jnp.bfloat16)
    acc = None
    for t in range(D_FF // D_MODEL):
        u = jnp.maximum(_f32dot(hb, w1_ref[:, t * D_MODEL:(t + 1) * D_MODEL]), 0.0)
        part = _f32dot((u * u).astype(jnp.bfloat16), w2_ref[t * D_MODEL:(t + 1) * D_MODEL, :])
        acc = part if acc is None else acc + part
    o_ref[...] = x + gt_ref[...] * acc


def _mlp(x, g, sc, sh, gt, w1, w2):
    s = x.shape[0]
    tm = min(MLP_TILE, s)
    row = lambda i: (i, 0)
    const = lambda i: (0, 0)
    vec = pl.BlockSpec((1, D_MODEL), const)
    return pl.pallas_call(
        _mlp_kernel,
        grid=(s // tm,),
        in_specs=[
            pl.BlockSpec((tm, D_MODEL), row),
            vec, vec, vec, vec,
            pl.BlockSpec((D_MODEL, D_FF), const, pipeline_mode=pl.Buffered(1)),
            pl.BlockSpec((D_FF, D_MODEL), const, pipeline_mode=pl.Buffered(1)),
        ],
        out_specs=pl.BlockSpec((tm, D_MODEL), row),
        out_shape=jax.ShapeDtypeStruct((s, D_MODEL), jnp.float32),
        compiler_params=pltpu.CompilerParams(
            dimension_semantics=("arbitrary",), vmem_limit_bytes=VMEM_LIMIT),
        name="relu2_mlp",
    )(x, g, sc, sh, gt, w1, w2)


def kernel(x, c, w_ada, b_ada, norm_mix_g, norm_ffn_g, w_in, b_igate, b_fgate, qn_g, kn_g, lam_qk,
           subln_g, rel_table, conv_w, conv_b, mhn_g, w_a, w_m, w_out, w_ff1, w_ff2):
    b, s, d = x.shape
    assert b == 1 and d == D_MODEL
    depth = w_ada.shape[0]
    bf = jnp.bfloat16
    xs = x[0]

    mod = _ada_modulation(c, w_ada, b_ada)
    bias_tiles = _attention_bias_tiles(rel_table, s)
    blk = A_HEAD_DIM
    bd = (jnp.arange(256)[:, None] // blk == jnp.arange(256)[None, :] // blk).astype(bf)

    gate_lo = 7 * D_MODEL
    gate_hi = gate_lo + 2 * M_HEADS
    for l in range(depth):
        lam_init = 0.8 - 0.6 * math.exp(-0.3 * l)
        sh_a, sc_a, gt_a, sh_f, sc_f, gt_f = [mod[l:l + 1, t * d:(t + 1) * d] for t in range(6)]

        w = w_in[l]
        w_main = jnp.concatenate([w[:, :gate_lo], w[:, gate_hi:]], axis=1).astype(bf)
        w_gate = jnp.pad(w[:, gate_lo:gate_hi], ((0, 0), (0, GATE_PAD - 2 * M_HEADS))).astype(bf)
        reps = A_WIDTH // A_HEAD_DIM
        qk_gain = jnp.stack([jnp.tile(qn_g[l], reps) * (A_HEAD_DIM ** -0.5 * LOG2E),
                             jnp.tile(kn_g[l], reps)])[:, None, :]

        proj, gates, qzt, vt = _in_projection(xs, norm_mix_g[l][None], sc_a, sh_a, w_main, w_gate,
                                              qk_gain, bd)

        ya = _diff_attention(qzt, proj, vt, bias_tiles, lam_qk[l], subln_g[l][:, None], lam_init)

        gates_t = gates[:, :2 * M_HEADS].T
        gate_bias = jnp.concatenate([b_igate[l], b_fgate[l]])[:, None]
        ym = _mlstm(proj, gates_t, gate_bias, conv_w[l], conv_b[l][None], mhn_g[l][None])

        xs = _merge(xs, ya, ym, proj, w_a[l].astype(bf), w_m[l].astype(bf), w_out[l].astype(bf), gt_a)
        xs = _mlp(xs, norm_ffn_g[l][None], sc_f, sh_f, gt_f, w_ff1[l].astype(bf), w_ff2[l].astype(bf))
    return xs[None]
```

```python
import functools
import math

import jax
import jax.numpy as jnp
from jax import lax
from jax.experimental import pallas as pl
from jax.experimental.pallas import tpu as pltpu

D_MODEL = 1024
A_HEADS = 8
A_HEAD_DIM = 64
A_WIDTH = A_HEADS * 2 * A_HEAD_DIM
M_HEADS = 4
M_HEAD_DIM = D_MODEL // M_HEADS
CONV_WIDTH = 4
NUM_BUCKETS = 32
MAX_DISTANCE = 128
D_FF = 4 * D_MODEL
EPS = 1e-6

LANES = 128
GATE_PAD = LANES
N_ROWMAJOR = 7 * D_MODEL
VMEM_LIMIT = 56 * 1024 * 1024
NEG_BIG = -1e30
LOG2E = math.log2(math.e)

ATT_BLOCK = 512
M_CHUNK = 256
M_HEADS_PER_STEP = 4
MLP_TILE = 512
STATE_W = M_HEAD_DIM + LANES


def _f32dot(a, b):
    return jnp.dot(a, b, preferred_element_type=jnp.float32)


def _modulated_norm(x, g, sc, sh):
    y = x * lax.rsqrt(jnp.mean(x * x, axis=-1, keepdims=True) + EPS)
    return (y * g) * (1.0 + sc) + sh


def _ada_kernel(c_ref, w_ref, b_ref, o_ref):
    c = c_ref[...]
    cond = c * jax.nn.sigmoid(c)
    o_ref[...] = jnp.sum(cond * w_ref[...], axis=0, keepdims=True) + b_ref[...]


def _ada_modulation(c, w_ada, b_ada):
    depth = w_ada.shape[0]
    nblk = w_ada.shape[2] // D_MODEL
    out = pl.pallas_call(
        _ada_kernel,
        grid=(depth, nblk),
        in_specs=[
            pl.BlockSpec((D_MODEL, 1), lambda l, j: (0, 0)),
            pl.BlockSpec((None, D_MODEL, D_MODEL), lambda l, j: (l, 0, j)),
            pl.BlockSpec((None, 1, D_MODEL), lambda l, j: (l, 0, j)),
        ],
        out_specs=pl.BlockSpec((None, 1, D_MODEL), lambda l, j: (l, 0, j)),
        out_shape=jax.ShapeDtypeStruct((depth, 1, w_ada.shape[2]), jnp.float32),
        name="ada_modulation",
    )(c.reshape(D_MODEL, 1), w_ada, b_ada[:, None, :])
    return out[:, 0, :]


def _proj_kernel(x_ref, g_ref, sc_ref, sh_ref, w_ref, wg_ref, qkg_ref, bd_ref,
                 o_ref, gate_ref, qzt_ref, vt_ref):
    tm = x_ref.shape[0]
    dh = A_HEAD_DIM
    h = _modulated_norm(x_ref[...], g_ref[...], sc_ref[...], sh_ref[...])
    hb = h.astype(jnp.bfloat16)
    gate_ref[...] = _f32dot(hb, wg_ref[...])
    bd = bd_ref[...]
    lane = lax.broadcasted_iota(jnp.int32, (tm, LANES), 1)
    for j in range(w_ref.shape[1] // D_MODEL):
        y = _f32dot(hb, w_ref[:, j * D_MODEL:(j + 1) * D_MODEL])
        if j == 2:
            for hd in range(A_HEADS):
                vt_ref[hd, 0] = y[:, hd * LANES:(hd + 1) * LANES].T.astype(vt_ref.dtype)
        elif j > 2:
            o_ref[:, (j - 2) * D_MODEL:(j - 1) * D_MODEL] = y.astype(o_ref.dtype)
        else:
            for t in range(D_MODEL // 256):
                yt = y[:, t * 256:(t + 1) * 256]
                ss = _f32dot((yt * yt).astype(jnp.bfloat16), bd)
                yn = yt * lax.rsqrt(ss * (1.0 / A_HEAD_DIM) + EPS) * qkg_ref[j, :, t * 256:(t + 1) * 256]
                if j == 1:
                    o_ref[:, t * 256:(t + 1) * 256] = yn.astype(o_ref.dtype)
                    continue
                for hh in range(256 // LANES):
                    hd = t * (256 // LANES) + hh
                    qh = yn[:, hh * LANES:(hh + 1) * LANES]
                    qzt_ref[hd, 0, :, :tm] = jnp.where(lane < dh, qh, 0.0).T.astype(qzt_ref.dtype)
                    qzt_ref[hd, 0, :, tm:] = jnp.where(lane >= dh, qh, 0.0).T.astype(qzt_ref.dtype)


def _in_projection(x, g, sc, sh, w_main, w_gate, qk_gain, bd):
    s = x.shape[0]
    tm = min(ATT_BLOCK, s)
    nt = s // tm
    const = lambda i: (0, 0)
    vec = pl.BlockSpec((1, D_MODEL), const)
    resident = pl.Buffered(1)
    return pl.pallas_call(
        _proj_kernel,
        grid=(nt,),
        in_specs=[
            pl.BlockSpec((tm, D_MODEL), lambda i: (i, 0)),
            vec, vec, vec,
            pl.BlockSpec((D_MODEL, w_main.shape[1]), const, pipeline_mode=resident),
            pl.BlockSpec((D_MODEL, GATE_PAD), const, pipeline_mode=resident),
            pl.BlockSpec((2, 1, D_MODEL), lambda i: (0, 0, 0)),
            pl.BlockSpec((256, 256), const),
        ],
        out_specs=[
            pl.BlockSpec((tm, N_ROWMAJOR), lambda i: (i, 0)),
            pl.BlockSpec((tm, GATE_PAD), lambda i: (i, 0)),
            pl.BlockSpec((A_HEADS, 1, LANES, 2 * tm), lambda i: (0, i, 0, 0)),
            pl.BlockSpec((A_HEADS, 1, LANES, tm), lambda i: (0, i, 0, 0)),
        ],
        out_shape=[
            jax.ShapeDtypeStruct((s, N_ROWMAJOR), jnp.bfloat16),
            jax.ShapeDtypeStruct((s, GATE_PAD), jnp.float32),
            jax.ShapeDtypeStruct((A_HEADS, nt, LANES, 2 * tm), jnp.bfloat16),
            jax.ShapeDtypeStruct((A_HEADS, nt, LANES, tm), jnp.bfloat16),
        ],
        compiler_params=pltpu.CompilerParams(
            dimension_semantics=("arbitrary",), vmem_limit_bytes=VMEM_LIMIT),
        name="in_projection",
    )(x, g, sc, sh, w_main, w_gate, qk_gain, bd)


def _attn_kernel(lam_init, qzt_ref, k_ref, vt_ref, ones_ref, bias_ref, lamqk_ref, g_ref, o_ref,
                 s0_scr, s1_scr, s2_scr, s3_scr, mx_scr, m_scr, acc_scr):
    i = pl.program_id(1)
    bq = qzt_ref.shape[1] // 2
    bk = bq
    dv = 2 * A_HEAD_DIM

    m_scr[...] = jnp.full(m_scr.shape, NEG_BIG, jnp.float32)
    acc_scr[...] = jnp.zeros(acc_scr.shape, jnp.float32)

    def logits(j, buf):
        s_ref, mx_ref = buf
        kb = k_ref[pl.ds(pl.multiple_of(j * bk, bk), bk), :]
        st = _f32dot(kb, qzt_ref[...])
        s_ref[...] = st
        mx_ref[...] = jnp.max(st, axis=0, keepdims=True)

    def consume(parts, bias_idx=None):
        tiles = []
        mx = None
        for j, (s_ref, mx_ref) in parts:
            st = s_ref[...]
            if bias_idx is None:
                part_mx = mx_ref[...]
            else:
                tile = bias_ref[bias_idx]
                st = st + jnp.concatenate([tile, tile], axis=1)
                part_mx = jnp.max(st, axis=0, keepdims=True)
            tiles.append((j, st))
            mx = part_mx if mx is None else jnp.maximum(mx, part_mx)
        m_old = m_scr[...]
        m_new = jnp.maximum(m_old, mx)
        pv = None
        for j, st in tiles:
            pt = jnp.exp2(st - m_new).astype(jnp.bfloat16)
            vt = jnp.concatenate([vt_ref[j], ones_ref[...]], axis=0)
            part = _f32dot(vt, pt)
            pv = part if pv is None else pv + part
        acc_scr[...] = acc_scr[...] * jnp.exp2(m_old - m_new) + pv
        m_scr[...] = m_new

    NEAR, DIAG = 1, 0
    n_far = jnp.maximum(i - 1, 0)
    buf_a, buf_b, buf_c, buf_d = [(s, mx_scr.at[n]) for n, s in enumerate((s0_scr, s1_scr, s2_scr, s3_scr))]
    logits(0, buf_a)

    @pl.when(i > 0)
    def _():
        logits(1, buf_b)

    def far_quad(j):
        logits(j + 2, buf_c)
        logits(j + 3, buf_d)
        consume([(j, buf_a), (j + 1, buf_b)])
        logits(j + 4, buf_a)
        logits(j + 5, buf_b)
        consume([(j + 2, buf_c), (j + 3, buf_d)])

    def far_octet(t, carry):
        far_quad(8 * t)
        far_quad(8 * t + 4)
        return carry

    lax.fori_loop(0, n_far // 8, far_octet, 0)

    @pl.when(n_far % 8 >= 4)
    def _():
        far_quad(8 * (n_far // 8))

    rest = n_far % 4
    jb = n_far - rest

    @pl.when(i == 0)
    def _():
        consume([(i, buf_a)], DIAG)

    @pl.when(jnp.logical_and(i > 0, rest == 0))
    def _():
        consume([(i - 1, buf_a)], NEAR)
        consume([(i, buf_b)], DIAG)

    @pl.when(rest == 1)
    def _():
        logits(i, buf_c)
        consume([(jb, buf_a)])
        consume([(i - 1, buf_b)], NEAR)
        consume([(i, buf_c)], DIAG)

    @pl.when(rest == 2)
    def _():
        logits(i - 1, buf_c)
        logits(i, buf_d)
        consume([(jb, buf_a), (jb + 1, buf_b)])
        consume([(i - 1, buf_c)], NEAR)
        consume([(i, buf_d)], DIAG)

    @pl.when(rest == 3)
    def _():
        logits(i - 2, buf_c)
        logits(i - 1, buf_d)
        consume([(jb, buf_a), (jb + 1, buf_b)])
        logits(i, buf_a)
        consume([(i - 2, buf_c)])
        consume([(i - 1, buf_d)], NEAR)
        consume([(i, buf_a)], DIAG)

    lq = lamqk_ref[...]
    e1 = jnp.exp(jnp.sum(lq[0:1] * lq[1:2], axis=1, keepdims=True))
    e2 = jnp.exp(jnp.sum(lq[2:3] * lq[3:4], axis=1, keepdims=True))
    lam = e1 - e2 + lam_init

    o = acc_scr[:dv, :] * (1.0 / acc_scr[dv:dv + 1, :])
    ot = o[:, :bq] - lam * o[:, bq:]
    ms = jnp.mean(ot * ot, axis=0, keepdims=True)
    on = ot * lax.rsqrt(ms + EPS) * g_ref[...] * (1.0 - lam_init)
    o_ref[...] = on.T.astype(o_ref.dtype)


def _diff_attention(qzt, proj, vt, bias_tiles, lam_qk, subln_col, lam_init):
    s = proj.shape[0]
    bq = min(ATT_BLOCK, s)
    nq = s // bq
    pad_rows = 16
    ones_rows = jnp.zeros((pad_rows, bq), jnp.bfloat16).at[0].set(1.0)
    return pl.pallas_call(
        functools.partial(_attn_kernel, lam_init),
        grid=(A_HEADS, nq),
        in_specs=[
            pl.BlockSpec((None, None, LANES, 2 * bq), lambda h, i: (h, i, 0, 0)),
            pl.BlockSpec((s, LANES), lambda h, i: (0, h)),
            pl.BlockSpec((None, nq, LANES, bq), lambda h, i: (h, 0, 0, 0)),
            pl.BlockSpec((pad_rows, bq), lambda h, i: (0, 0)),
            pl.BlockSpec((None, 2, bq, bq), lambda h, i: (h, 0, 0, 0)),
            pl.BlockSpec((4, A_HEAD_DIM), lambda h, i: (0, 0)),
            pl.BlockSpec((LANES, 1), lambda h, i: (0, 0)),
        ],
        out_specs=pl.BlockSpec((bq, LANES), lambda h, i: (i, h)),
        out_shape=jax.ShapeDtypeStruct((s, A_WIDTH), jnp.bfloat16),
        scratch_shapes=[
            pltpu.VMEM((bq, 2 * bq), jnp.float32),
            pltpu.VMEM((bq, 2 * bq), jnp.float32),
            pltpu.VMEM((bq, 2 * bq), jnp.float32),
            pltpu.VMEM((bq, 2 * bq), jnp.float32),
            pltpu.VMEM((4, 1, 2 * bq), jnp.float32),
            pltpu.VMEM((1, 2 * bq), jnp.float32),
            pltpu.VMEM((LANES + pad_rows, 2 * bq), jnp.float32),
        ],
        compiler_params=pltpu.CompilerParams(
            dimension_semantics=("arbitrary", "arbitrary"), vmem_limit_bytes=VMEM_LIMIT),
        name="diff_attention",
    )(qzt, proj, vt, ones_rows, bias_tiles, lam_qk, subln_col)


def _t5_bucket(dist):
    max_exact = NUM_BUCKETS // 2
    nf = jnp.maximum(dist, 1).astype(jnp.float32)
    large = max_exact + (jnp.log(nf / max_exact) / math.log(MAX_DISTANCE / max_exact)
                         * (NUM_BUCKETS - max_exact)).astype(jnp.int32)
    large = jnp.minimum(large, NUM_BUCKETS - 1)
    return jnp.where(dist < max_exact, dist, large)


def _attention_bias_tiles(rel_table, s):
    bq = min(ATT_BLOCK, s)
    dists = jnp.arange(2 * bq)
    bias = rel_table[_t5_bucket(dists)].T.astype(jnp.float32)
    far = rel_table[_t5_bucket(jnp.array([s - 1]))].T.astype(jnp.float32)
    bias = (bias - far) * LOG2E
    period = 2 * bq + 1
    masked = jnp.full((A_HEADS, bq + 1), NEG_BIG, jnp.float32)
    u_diag = jnp.concatenate([bias[:, :bq], masked], axis=1)
    u_near = jnp.concatenate([bias[:, bq:], jnp.zeros((A_HEADS, 1), jnp.float32), bias[:, :bq]], axis=1)
    u = jnp.stack([u_diag, u_near], axis=1)
    flat = jnp.tile(u, (1, 1, bq))[:, :, :bq * 2 * bq]
    return flat.reshape(A_HEADS, 2, bq, 2 * bq)[..., :bq]


def _mlstm_head(hd, c, q_ref, k_ref, v_ref, o_ref, gate_ref, gbias_ref, cwq_ref, cwk_ref, cbq_ref, cbk_ref,
                shift_ref, hg_ref, y_ref, state_scr, m_scr, pq_scr, pk_scr):
    L = q_ref.shape[0]
    dh = M_HEAD_DIM

    @pl.when(c == 0)
    def _():
        for scr in (state_scr, m_scr, pq_scr, pk_scr):
            scr[...] = jnp.zeros(scr.shape, jnp.float32)

    row8 = lax.broadcasted_iota(jnp.int32, (8, dh), 0)

    def conv_silu(u_ref, tail_scr, w_ref, b_ref):
        ub = u_ref[...]
        u = ub.astype(jnp.float32)
        w = w_ref[...]
        taps = _f32dot(shift_ref[...], ub)
        tail = tail_scr[...]
        y = b_ref[...] + w[CONV_WIDTH - 1:CONV_WIDTH] * u
        top = jnp.zeros((8, dh), jnp.float32)
        for back in range(1, CONV_WIDTH):
            wb = w[CONV_WIDTH - 1 - back:CONV_WIDTH - back]
            y = y + wb * taps[(back - 1) * L:back * L]
            top = top + wb * jnp.where(row8 < back, pltpu.roll(tail, back, 0), 0.0)
        y = jnp.concatenate([y[:8] + top, y[8:]], axis=0)
        tail_scr[...] = u[L - 8:]
        half = 0.5 * y
        return half + half * jnp.tanh(half)

    qc = conv_silu(q_ref, pq_scr, cwq_ref, cbq_ref)
    kc = conv_silu(k_ref, pk_scr, cwk_ref, cbk_ref) * (dh ** -0.5)
    qb = qc.astype(jnp.bfloat16)
    kb = kc.astype(jnp.bfloat16)

    gb = gbias_ref[...]
    gates = gate_ref[...] + gb
    sub = lax.broadcasted_iota(jnp.int32, gates.shape, 0)
    ig_row = jnp.sum(jnp.where(sub == hd, gates, 0.0), axis=0, keepdims=True)
    fg_row = jnp.sum(jnp.where(sub == hd + M_HEADS, gates, 0.0), axis=0, keepdims=True)
    lf_row = jax.nn.log_sigmoid(fg_row)

    t_idx = lax.broadcasted_iota(jnp.int32, (L, L), 0)
    s_idx = lax.broadcasted_iota(jnp.int32, (L, L), 1)
    tril = s_idx <= t_idx
    eye = s_idx == t_idx

    def to_col(r):
        return jnp.sum(jnp.where(eye, r, 0.0), axis=1, keepdims=True)

    def to_row(col):
        return jnp.sum(jnp.where(eye, col, 0.0), axis=0, keepdims=True)

    bcum_col = jnp.sum(jnp.where(tril, lf_row, 0.0), axis=1, keepdims=True)
    bcum_row = to_row(bcum_col)
    ig_col = to_col(ig_row)
    b_last = bcum_col[L - 1:L, :]
    m_prev = m_scr[0:1, 0:1]

    dmat = jnp.where(tril, bcum_col - bcum_row + ig_row, NEG_BIG)
    inter = bcum_col + m_prev
    m_t = jnp.maximum(jnp.max(dmat, axis=1, keepdims=True), inter)
    qk = lax.dot_general(qb, kb, (((1,), (1,)), ((), ())), preferred_element_type=jnp.float32)
    scores = qk * jnp.exp(dmat - m_t)
    a_inter = jnp.exp(inter - m_t)

    lane_w = lax.broadcasted_iota(jnp.int32, (L, LANES), 1)
    v_ext = jnp.concatenate(
        [v_ref[...], jnp.where(lane_w == 0, 1.0, 0.0).astype(jnp.bfloat16)], axis=1)
    state = state_scr[...]
    res = _f32dot(scores.astype(jnp.bfloat16), v_ext) + a_inter * _f32dot(qb, state.astype(jnp.bfloat16))
    num = res[:, :dh]
    den = res[:, dh:dh + 1]
    h = num / jnp.maximum(jnp.abs(den), jnp.exp(-m_t))

    g_col = b_last - bcum_col + ig_col
    m_new = jnp.maximum(b_last + m_prev, jnp.max(g_col, axis=0, keepdims=True))
    wk = jnp.exp(g_col - m_new)
    decay = jnp.exp(b_last + m_prev - m_new)
    kw = (kc * wk).astype(jnp.bfloat16)
    upd = lax.dot_general(kw, v_ext, (((0,), (0,)), ((), ())), preferred_element_type=jnp.float32)
    state_scr[...] = decay * state + upd
    m_scr[...] = jnp.broadcast_to(m_new, m_scr.shape)

    hn = h * lax.rsqrt(jnp.mean(h * h, axis=-1, keepdims=True) + EPS) * hg_ref[...]
    o_gate = 0.5 + 0.5 * jnp.tanh(0.5 * o_ref[...].astype(jnp.float32))
    y_ref[...] = (o_gate * hn).astype(y_ref.dtype)


def _mlstm_kernel(q_ref, k_ref, v_ref, o_ref, gate_ref, gbias_ref, cwq_ref, cwk_ref, cbq_ref, cbk_ref,
                  shift_ref, hg_ref, y_ref, state_scr, m_scr, pq_scr, pk_scr):
    dh = M_HEAD_DIM
    for hh in range(M_HEADS_PER_STEP):
        cols = pl.ds(hh * dh, dh)
        _mlstm_head(pl.program_id(0) * M_HEADS_PER_STEP + hh, pl.program_id(1),
                    q_ref.at[:, cols], k_ref.at[:, cols], v_ref.at[:, cols], o_ref.at[:, cols],
                    gate_ref, gbias_ref, cwq_ref.at[:, cols], cwk_ref.at[:, cols],
                    cbq_ref.at[:, cols], cbk_ref.at[:, cols], shift_ref, hg_ref, y_ref.at[:, cols],
                    state_scr.at[hh], m_scr.at[hh], pq_scr.at[hh], pk_scr.at[hh])


def _mlstm(proj, gates_t, gate_bias, conv_w, conv_b, mhn_g):
    s = proj.shape[0]
    L = min(M_CHUNK, s)
    nc = s // L
    dh = M_HEAD_DIM
    hps = M_HEADS_PER_STEP
    w = hps * dh
    ng = M_HEADS // hps
    base = D_MODEL // w
    taps = jnp.arange((CONV_WIDTH - 1) * L)
    shift = (jnp.arange(L)[None, :] == (taps % L - taps // L - 1)[:, None]).astype(jnp.bfloat16)
    return pl.pallas_call(
        _mlstm_kernel,
        grid=(ng, nc),
        in_specs=[
            pl.BlockSpec((L, w), lambda g, c: (c, base + g)),
            pl.BlockSpec((L, w), lambda g, c: (c, base + ng + g)),
            pl.BlockSpec((L, w), lambda g, c: (c, base + 2 * ng + g)),
            pl.BlockSpec((L, w), lambda g, c: (c, base + 3 * ng + g)),
            pl.BlockSpec((2 * M_HEADS, L), lambda g, c: (0, c)),
            pl.BlockSpec((2 * M_HEADS, 1), lambda g, c: (0, 0)),
            pl.BlockSpec((CONV_WIDTH, w), lambda g, c: (0, g)),
            pl.BlockSpec((CONV_WIDTH, w), lambda g, c: (0, ng + g)),
            pl.BlockSpec((1, w), lambda g, c: (0, g)),
            pl.BlockSpec((1, w), lambda g, c: (0, ng + g)),
            pl.BlockSpec(((CONV_WIDTH - 1) * L, L), lambda g, c: (0, 0)),
            pl.BlockSpec((1, dh), lambda g, c: (0, 0)),
        ],
        out_specs=pl.BlockSpec((L, w), lambda g, c: (c, g)),
        out_shape=jax.ShapeDtypeStruct((s, D_MODEL), jnp.bfloat16),
        scratch_shapes=[
            pltpu.VMEM((hps, dh, STATE_W), jnp.float32),
            pltpu.VMEM((hps, 8, LANES), jnp.float32),
            pltpu.VMEM((hps, 8, dh), jnp.float32),
            pltpu.VMEM((hps, 8, dh), jnp.float32),
        ],
        compiler_params=pltpu.CompilerParams(
            dimension_semantics=("arbitrary", "arbitrary"), vmem_limit_bytes=VMEM_LIMIT),
        name="mlstm",
    )(proj, proj, proj, proj, gates_t, gate_bias, conv_w, conv_w, conv_b, conv_b, shift, mhn_g)


def _merge_kernel(x_ref, ya_ref, ym_ref, ga_ref, gm_ref, wa_ref, wm_ref, wo_ref, gt_ref, o_ref):
    a = _f32dot(ya_ref[...], wa_ref[...])
    b = _f32dot(ym_ref[...], wm_ref[...])
    merged = (jax.nn.sigmoid(ga_ref[...].astype(jnp.float32)) * a
              + jax.nn.sigmoid(gm_ref[...].astype(jnp.float32)) * b)
    o_ref[...] = x_ref[...] + gt_ref[...] * _f32dot(merged.astype(jnp.bfloat16), wo_ref[...])


def _merge(x, ya, ym, proj, w_a, w_m, w_out, gt):
    s = x.shape[0]
    tm = min(MLP_TILE, s)
    row = lambda i: (i, 0)
    const = lambda i: (0, 0)
    wspec = pl.BlockSpec((D_MODEL, D_MODEL), const, pipeline_mode=pl.Buffered(1))
    return pl.pallas_call(
        _merge_kernel,
        grid=(s // tm,),
        in_specs=[
            pl.BlockSpec((tm, D_MODEL), row),
            pl.BlockSpec((tm, D_MODEL), row),
            pl.BlockSpec((tm, D_MODEL), row),
            pl.BlockSpec((tm, D_MODEL), lambda i: (i, 5)),
            pl.BlockSpec((tm, D_MODEL), lambda i: (i, 6)),
            wspec, wspec, wspec,
            pl.BlockSpec((1, D_MODEL), const),
        ],
        out_specs=pl.BlockSpec((tm, D_MODEL), row),
        out_shape=jax.ShapeDtypeStruct((s, D_MODEL), jnp.float32),
        compiler_params=pltpu.CompilerParams(
            dimension_semantics=("arbitrary",), vmem_limit_bytes=VMEM_LIMIT),
        name="merge_out_projection",
    )(x, ya, ym, proj, proj, w_a, w_m, w_out, gt)


def _mlp_kernel(x_ref, g_ref, sc_ref, sh_ref, gt_ref, w1_ref, w2_ref, o_ref):
    x = x_ref[...]
    hb = _modulated_norm(x, g_ref[...], sc_ref[...], sh_ref[...]).astype(jnp.bfloat16)
    acc = None
    for t in range(D_FF // D_MODEL):
        u = jnp.maximum(_f32dot(hb, w1_ref[:, t * D_MODEL:(t + 1) * D_MODEL]), 0.0)
        part = _f32dot((u * u).astype(jnp.bfloat16), w2_ref[t * D_MODEL:(t + 1) * D_MODEL, :])
        acc = part if acc is None else acc + part
    o_ref[...] = x + gt_ref[...] * acc


def _mlp(x, g, sc, sh, gt, w1, w2):
    s = x.shape[0]
    tm = min(MLP_TILE, s)
    row = lambda i: (i, 0)
    const = lambda i: (0, 0)
    vec = pl.BlockSpec((1, D_MODEL), const)
    return pl.pallas_call(
        _mlp_kernel,
        grid=(s // tm,),
        in_specs=[
            pl.BlockSpec((tm, D_MODEL), row),
            vec, vec, vec, vec,
            pl.BlockSpec((D_MODEL, D_FF), const, pipeline_mode=pl.Buffered(1)),
            pl.BlockSpec((D_FF, D_MODEL), const, pipeline_mode=pl.Buffered(1)),
        ],
        out_specs=pl.BlockSpec((tm, D_MODEL), row),
        out_shape=jax.ShapeDtypeStruct((s, D_MODEL), jnp.float32),
        compiler_params=pltpu.CompilerParams(
            dimension_semantics=("arbitrary",), vmem_limit_bytes=VMEM_LIMIT),
        name="relu2_mlp",
    )(x, g, sc, sh, gt, w1, w2)


def kernel(x, c, w_ada, b_ada, norm_mix_g, norm_ffn_g, w_in, b_igate, b_fgate, qn_g, kn_g, lam_qk,
           subln_g, rel_table, conv_w, conv_b, mhn_g, w_a, w_m, w_out, w_ff1, w_ff2):
    b, s, d = x.shape
    assert b == 1 and d == D_MODEL
    depth = w_ada.shape[0]
    bf = jnp.bfloat16
    xs = x[0]

    mod = _ada_modulation(c, w_ada, b_ada)
    bias_tiles = _attention_bias_tiles(rel_table, s)
    blk = A_HEAD_DIM
    bd = (jnp.arange(256)[:, None] // blk == jnp.arange(256)[None, :] // blk).astype(bf)

    gate_lo = 7 * D_MODEL
    gate_hi = gate_lo + 2 * M_HEADS
    for l in range(depth):
        lam_init = 0.8 - 0.6 * math.exp(-0.3 * l)
        sh_a, sc_a, gt_a, sh_f, sc_f, gt_f = [mod[l:l + 1, t * d:(t + 1) * d] for t in range(6)]

        w = w_in[l]
        w_main = jnp.concatenate([w[:, :gate_lo], w[:, gate_hi:]], axis=1).astype(bf)
        w_gate = jnp.pad(w[:, gate_lo:gate_hi], ((0, 0), (0, GATE_PAD - 2 * M_HEADS))).astype(bf)
        reps = A_WIDTH // A_HEAD_DIM
        qk_gain = jnp.stack([jnp.tile(qn_g[l], reps) * (A_HEAD_DIM ** -0.5 * LOG2E),
                             jnp.tile(kn_g[l], reps)])[:, None, :]

        proj, gates, qzt, vt = _in_projection(xs, norm_mix_g[l][None], sc_a, sh_a, w_main, w_gate,
                                              qk_gain, bd)

        ya = _diff_attention(qzt, proj, vt, bias_tiles, lam_qk[l], subln_g[l][:, None], lam_init)

        gates_t = gates[:, :2 * M_HEADS].T
        gate_bias = jnp.concatenate([b_igate[l], b_fgate[l]])[:, None]
        ym = _mlstm(proj, gates_t, gate_bias, conv_w[l], conv_b[l][None], mhn_g[l][None])

        xs = _merge(xs, ya, ym, proj, w_a[l].astype(bf), w_m[l].astype(bf), w_out[l].astype(bf), gt_a)
        xs = _mlp(xs, norm_ffn_g[l][None], sc_f, sh_f, gt_f, w_ff1[l].astype(bf), w_ff2[l].astype(bf))
    return xs[None]
```

```python
import functools
import math

import jax
import jax.numpy as jnp
from jax import lax
from jax.experimental import pallas as pl
from jax.experimental.pallas import tpu as pltpu

D_MODEL = 1024
A_HEADS = 8
A_HEAD_DIM = 64
A_WIDTH = A_HEADS * 2 * A_HEAD_DIM
M_HEADS = 4
M_HEAD_DIM = D_MODEL // M_HEADS
CONV_WIDTH = 4
NUM_BUCKETS = 32
MAX_DISTANCE = 128
D_FF = 4 * D_MODEL
EPS = 1e-6

LANES = 128
GATE_PAD = LANES
N_ROWMAJOR = 7 * D_MODEL
VMEM_LIMIT = 56 * 1024 * 1024
NEG_BIG = -1e30
LOG2E = math.log2(math.e)

ATT_BLOCK = 512
M_CHUNK = 512
M_HEADS_PER_STEP = 2
MLP_TILE = 512
STATE_W = M_HEAD_DIM + LANES


def _f32dot(a, b):
    return jnp.dot(a, b, preferred_element_type=jnp.float32)


def _modulated_norm(x, g, sc, sh):
    y = x * lax.rsqrt(jnp.mean(x * x, axis=-1, keepdims=True) + EPS)
    return (y * g) * (1.0 + sc) + sh


def _ada_kernel(c_ref, w_ref, b_ref, o_ref):
    c = c_ref[...]
    cond = c * jax.nn.sigmoid(c)
    o_ref[...] = jnp.sum(cond * w_ref[...], axis=0, keepdims=True) + b_ref[...]


def _ada_modulation(c, w_ada, b_ada):
    depth = w_ada.shape[0]
    nblk = w_ada.shape[2] // D_MODEL
    out = pl.pallas_call(
        _ada_kernel,
        grid=(depth, nblk),
        in_specs=[
            pl.BlockSpec((D_MODEL, 1), lambda l, j: (0, 0)),
            pl.BlockSpec((None, D_MODEL, D_MODEL), lambda l, j: (l, 0, j)),
            pl.BlockSpec((None, 1, D_MODEL), lambda l, j: (l, 0, j)),
        ],
        out_specs=pl.BlockSpec((None, 1, D_MODEL), lambda l, j: (l, 0, j)),
        out_shape=jax.ShapeDtypeStruct((depth, 1, w_ada.shape[2]), jnp.float32),
        name="ada_modulation",
    )(c.reshape(D_MODEL, 1), w_ada, b_ada[:, None, :])
    return out[:, 0, :]


def _proj_kernel(x_ref, g_ref, sc_ref, sh_ref, w_ref, wg_ref, qkg_ref, bd_ref,
                 o_ref, gate_ref, qzt_ref, vt_ref):
    tm = x_ref.shape[0]
    dh = A_HEAD_DIM
    h = _modulated_norm(x_ref[...], g_ref[...], sc_ref[...], sh_ref[...])
    hb = h.astype(jnp.bfloat16)
    gate_ref[...] = _f32dot(hb, wg_ref[...])
    bd = bd_ref[...]
    lane = lax.broadcasted_iota(jnp.int32, (tm, LANES), 1)
    for j in range(w_ref.shape[1] // D_MODEL):
        y = _f32dot(hb, w_ref[:, j * D_MODEL:(j + 1) * D_MODEL])
        if j == 2:
            for hd in range(A_HEADS):
                vt_ref[hd, 0] = y[:, hd * LANES:(hd + 1) * LANES].T.astype(vt_ref.dtype)
        elif j > 2:
            o_ref[:, (j - 2) * D_MODEL:(j - 1) * D_MODEL] = y.astype(o_ref.dtype)
        else:
            for t in range(D_MODEL // 256):
                yt = y[:, t * 256:(t + 1) * 256]
                ss = _f32dot((yt * yt).astype(jnp.bfloat16), bd)
                yn = yt * lax.rsqrt(ss * (1.0 / A_HEAD_DIM) + EPS) * qkg_ref[j, :, t * 256:(t + 1) * 256]
                if j == 1:
                    o_ref[:, t * 256:(t + 1) * 256] = yn.astype(o_ref.dtype)
                    continue
                for hh in range(256 // LANES):
                    hd = t * (256 // LANES) + hh
                    qh = yn[:, hh * LANES:(hh + 1) * LANES]
                    qzt_ref[hd, 0, :, :tm] = jnp.where(lane < dh, qh, 0.0).T.astype(qzt_ref.dtype)
                    qzt_ref[hd, 0, :, tm:] = jnp.where(lane >= dh, qh, 0.0).T.astype(qzt_ref.dtype)


def _in_projection(x, g, sc, sh, w_main, w_gate, qk_gain, bd):
    s = x.shape[0]
    tm = min(ATT_BLOCK, s)
    nt = s // tm
    const = lambda i: (0, 0)
    vec = pl.BlockSpec((1, D_MODEL), const)
    resident = pl.Buffered(1)
    return pl.pallas_call(
        _proj_kernel,
        grid=(nt,),
        in_specs=[
            pl.BlockSpec((tm, D_MODEL), lambda i: (i, 0)),
            vec, vec, vec,
            pl.BlockSpec((D_MODEL, w_main.shape[1]), const, pipeline_mode=resident),
            pl.BlockSpec((D_MODEL, GATE_PAD), const, pipeline_mode=resident),
            pl.BlockSpec((2, 1, D_MODEL), lambda i: (0, 0, 0)),
            pl.BlockSpec((256, 256), const),
        ],
        out_specs=[
            pl.BlockSpec((tm, N_ROWMAJOR), lambda i: (i, 0)),
            pl.BlockSpec((tm, GATE_PAD), lambda i: (i, 0)),
            pl.BlockSpec((A_HEADS, 1, LANES, 2 * tm), lambda i: (0, i, 0, 0)),
            pl.BlockSpec((A_HEADS, 1, LANES, tm), lambda i: (0, i, 0, 0)),
        ],
        out_shape=[
            jax.ShapeDtypeStruct((s, N_ROWMAJOR), jnp.bfloat16),
            jax.ShapeDtypeStruct((s, GATE_PAD), jnp.float32),
            jax.ShapeDtypeStruct((A_HEADS, nt, LANES, 2 * tm), jnp.bfloat16),
            jax.ShapeDtypeStruct((A_HEADS, nt, LANES, tm), jnp.bfloat16),
        ],
        compiler_params=pltpu.CompilerParams(
            dimension_semantics=("arbitrary",), vmem_limit_bytes=VMEM_LIMIT),
        name="in_projection",
    )(x, g, sc, sh, w_main, w_gate, qk_gain, bd)


def _attn_kernel(lam_init, qzt_ref, k_ref, vt_ref, ones_ref, bias_ref, lamqk_ref, g_ref, o_ref,
                 sa_scr, sb_scr, mxa_scr, mxb_scr, m_scr, acc_scr):
    i = pl.program_id(1)
    bq = qzt_ref.shape[1] // 2
    bk = bq
    dv = 2 * A_HEAD_DIM

    m_scr[...] = jnp.full(m_scr.shape, NEG_BIG, jnp.float32)
    acc_scr[...] = jnp.zeros(acc_scr.shape, jnp.float32)

    def logits(j, s_ref, mx_ref):
        kb = k_ref[pl.ds(pl.multiple_of(j * bk, bk), bk), :]
        st = _f32dot(kb, qzt_ref[...])
        s_ref[...] = st
        mx_ref[...] = jnp.max(st, axis=0, keepdims=True)

    def consume(j, s_ref, mx_ref, bias_idx):
        st = s_ref[...]
        if bias_idx is None:
            mx = mx_ref[...]
        else:
            tile = bias_ref[bias_idx]
            st = st + jnp.concatenate([tile, tile], axis=1)
            mx = jnp.max(st, axis=0, keepdims=True)
        m_old = m_scr[...]
        m_new = jnp.maximum(m_old, mx)
        pt = jnp.exp2(st - m_new).astype(jnp.bfloat16)
        vt = jnp.concatenate([vt_ref[j], ones_ref[...]], axis=0)
        pv = _f32dot(vt, pt)
        acc_scr[...] = acc_scr[...] * jnp.exp2(m_old - m_new) + pv
        m_scr[...] = m_new

    NEAR, DIAG = 1, 0
    n_far = jnp.maximum(i - 1, 0)
    buf_a = (sa_scr, mxa_scr)
    buf_b = (sb_scr, mxb_scr)
    logits(0, *buf_a)

    def far_pair(j):
        logits(j + 1, *buf_b)
        consume(j, *buf_a, None)
        logits(j + 2, *buf_a)
        consume(j + 1, *buf_b, None)

    def far_octet(t, carry):
        for pair in range(4):
            far_pair(8 * t + 2 * pair)
        return carry

    lax.fori_loop(0, n_far // 8, far_octet, 0)

    @pl.when(n_far % 8 >= 4)
    def _():
        far_pair(8 * (n_far // 8))
        far_pair(8 * (n_far // 8) + 2)

    @pl.when(n_far % 4 >= 2)
    def _():
        far_pair(4 * (n_far // 4))

    @pl.when(i == 0)
    def _():
        consume(i, *buf_a, DIAG)

    @pl.when(jnp.logical_and(i > 0, n_far % 2 == 0))
    def _():
        logits(i, *buf_b)
        consume(i - 1, *buf_a, NEAR)
        consume(i, *buf_b, DIAG)

    @pl.when(n_far % 2 == 1)
    def _():
        logits(i - 1, *buf_b)
        consume(i - 2, *buf_a, None)
        logits(i, *buf_a)
        consume(i - 1, *buf_b, NEAR)
        consume(i, *buf_a, DIAG)

    lq = lamqk_ref[...]
    e1 = jnp.exp(jnp.sum(lq[0:1] * lq[1:2], axis=1, keepdims=True))
    e2 = jnp.exp(jnp.sum(lq[2:3] * lq[3:4], axis=1, keepdims=True))
    lam = e1 - e2 + lam_init

    o = acc_scr[:dv, :] * (1.0 / acc_scr[dv:dv + 1, :])
    ot = o[:, :bq] - lam * o[:, bq:]
    ms = jnp.mean(ot * ot, axis=0, keepdims=True)
    on = ot * lax.rsqrt(ms + EPS) * g_ref[...] * (1.0 - lam_init)
    o_ref[...] = on.T.astype(o_ref.dtype)


def _diff_attention(qzt, proj, vt, bias_tiles, lam_qk, subln_col, lam_init):
    s = proj.shape[0]
    bq = min(ATT_BLOCK, s)
    nq = s // bq
    pad_rows = 16
    ones_rows = jnp.zeros((pad_rows, bq), jnp.bfloat16).at[0].set(1.0)
    return pl.pallas_call(
        functools.partial(_attn_kernel, lam_init),
        grid=(A_HEADS, nq),
        in_specs=[
            pl.BlockSpec((None, None, LANES, 2 * bq), lambda h, i: (h, i, 0, 0)),
            pl.BlockSpec((s, LANES), lambda h, i: (0, h)),
            pl.BlockSpec((None, nq, LANES, bq), lambda h, i: (h, 0, 0, 0)),
            pl.BlockSpec((pad_rows, bq), lambda h, i: (0, 0)),
            pl.BlockSpec((None, 2, bq, bq), lambda h, i: (h, 0, 0, 0)),
            pl.BlockSpec((4, A_HEAD_DIM), lambda h, i: (0, 0)),
            pl.BlockSpec((LANES, 1), lambda h, i: (0, 0)),
        ],
        out_specs=pl.BlockSpec((bq, LANES), lambda h, i: (i, h)),
        out_shape=jax.ShapeDtypeStruct((s, A_WIDTH), jnp.bfloat16),
        scratch_shapes=[
            pltpu.VMEM((bq, 2 * bq), jnp.float32),
            pltpu.VMEM((bq, 2 * bq), jnp.float32),
            pltpu.VMEM((1, 2 * bq), jnp.float32),
            pltpu.VMEM((1, 2 * bq), jnp.float32),
            pltpu.VMEM((1, 2 * bq), jnp.float32),
            pltpu.VMEM((LANES + pad_rows, 2 * bq), jnp.float32),
        ],
        compiler_params=pltpu.CompilerParams(
            dimension_semantics=("arbitrary", "arbitrary"), vmem_limit_bytes=VMEM_LIMIT),
        name="diff_attention",
    )(qzt, proj, vt, ones_rows, bias_tiles, lam_qk, subln_col)


def _t5_bucket(dist):
    max_exact = NUM_BUCKETS // 2
    nf = jnp.maximum(dist, 1).astype(jnp.float32)
    large = max_exact + (jnp.log(nf / max_exact) / math.log(MAX_DISTANCE / max_exact)
                         * (NUM_BUCKETS - max_exact)).astype(jnp.int32)
    large = jnp.minimum(large, NUM_BUCKETS - 1)
    return jnp.where(dist < max_exact, dist, large)


def _attention_bias_tiles(rel_table, s):
    bq = min(ATT_BLOCK, s)
    dists = jnp.arange(2 * bq)
    bias = rel_table[_t5_bucket(dists)].T.astype(jnp.float32)
    far = rel_table[_t5_bucket(jnp.array([s - 1]))].T.astype(jnp.float32)
    bias = (bias - far) * LOG2E
    period = 2 * bq + 1
    masked = jnp.full((A_HEADS, bq + 1), NEG_BIG, jnp.float32)
    u_diag = jnp.concatenate([bias[:, :bq], masked], axis=1)
    u_near = jnp.concatenate([bias[:, bq:], jnp.zeros((A_HEADS, 1), jnp.float32), bias[:, :bq]], axis=1)
    u = jnp.stack([u_diag, u_near], axis=1)
    flat = jnp.tile(u, (1, 1, bq))[:, :, :bq * 2 * bq]
    return flat.reshape(A_HEADS, 2, bq, 2 * bq)[..., :bq]


def _mlstm_head(hd, c, q_ref, k_ref, v_ref, o_ref, gate_ref, gbias_ref, cwq_ref, cwk_ref, cbq_ref, cbk_ref,
                shift_ref, hg_ref, y_ref, state_scr, m_scr, pq_scr, pk_scr):
    L = q_ref.shape[0]
    dh = M_HEAD_DIM

    @pl.when(c == 0)
    def _():
        for scr in (state_scr, m_scr, pq_scr, pk_scr):
            scr[...] = jnp.zeros(scr.shape, jnp.float32)

    row8 = lax.broadcasted_iota(jnp.int32, (8, dh), 0)

    def conv_silu(u_ref, tail_scr, w_ref, b_ref):
        ub = u_ref[...]
        u = ub.astype(jnp.float32)
        w = w_ref[...]
        taps = _f32dot(shift_ref[...], ub)
        tail = tail_scr[...]
        y = b_ref[...] + w[CONV_WIDTH - 1:CONV_WIDTH] * u
        top = jnp.zeros((8, dh), jnp.float32)
        for back in range(1, CONV_WIDTH):
            wb = w[CONV_WIDTH - 1 - back:CONV_WIDTH - back]
            y = y + wb * taps[(back - 1) * L:back * L]
            top = top + wb * jnp.where(row8 < back, pltpu.roll(tail, back, 0), 0.0)
        y = jnp.concatenate([y[:8] + top, y[8:]], axis=0)
        tail_scr[...] = u[L - 8:]
        half = 0.5 * y
        return half + half * jnp.tanh(half)

    qc = conv_silu(q_ref, pq_scr, cwq_ref, cbq_ref)
    kc = conv_silu(k_ref, pk_scr, cwk_ref, cbk_ref) * (dh ** -0.5)
    qb = qc.astype(jnp.bfloat16)
    kb = kc.astype(jnp.bfloat16)

    gb = gbias_ref[...]
    gates = gate_ref[...] + gb
    sub = lax.broadcasted_iota(jnp.int32, gates.shape, 0)
    ig_row = jnp.sum(jnp.where(sub == hd, gates, 0.0), axis=0, keepdims=True)
    fg_row = jnp.sum(jnp.where(sub == hd + M_HEADS, gates, 0.0), axis=0, keepdims=True)
    lf_row = jax.nn.log_sigmoid(fg_row)

    t_idx = lax.broadcasted_iota(jnp.int32, (L, L), 0)
    s_idx = lax.broadcasted_iota(jnp.int32, (L, L), 1)
    tril = s_idx <= t_idx
    eye = s_idx == t_idx

    def to_col(r):
        return jnp.sum(jnp.where(eye, r, 0.0), axis=1, keepdims=True)

    def to_row(col):
        return jnp.sum(jnp.where(eye, col, 0.0), axis=0, keepdims=True)

    bcum_col = jnp.sum(jnp.where(tril, lf_row, 0.0), axis=1, keepdims=True)
    bcum_row = to_row(bcum_col)
    ig_col = to_col(ig_row)
    b_last = bcum_col[L - 1:L, :]
    m_prev = m_scr[0:1, 0:1]

    dmat = jnp.where(tril, bcum_col - bcum_row + ig_row, NEG_BIG)
    inter = bcum_col + m_prev
    m_t = jnp.maximum(jnp.max(dmat, axis=1, keepdims=True), inter)
    qk = lax.dot_general(qb, kb, (((1,), (1,)), ((), ())), preferred_element_type=jnp.float32)
    scores = qk * jnp.exp(dmat - m_t)
    a_inter = jnp.exp(inter - m_t)

    lane_w = lax.broadcasted_iota(jnp.int32, (L, LANES), 1)
    v_ext = jnp.concatenate(
        [v_ref[...], jnp.where(lane_w == 0, 1.0, 0.0).astype(jnp.bfloat16)], axis=1)
    state = state_scr[...]
    res = _f32dot(scores.astype(jnp.bfloat16), v_ext) + a_inter * _f32dot(qb, state.astype(jnp.bfloat16))
    num = res[:, :dh]
    den = res[:, dh:dh + 1]
    h = num / jnp.maximum(jnp.abs(den), jnp.exp(-m_t))

    g_col = b_last - bcum_col + ig_col
    m_new = jnp.maximum(b_last + m_prev, jnp.max(g_col, axis=0, keepdims=True))
    wk = jnp.exp(g_col - m_new)
    decay = jnp.exp(b_last + m_prev - m_new)
    kw = (kc * wk).astype(jnp.bfloat16)
    upd = lax.dot_general(kw, v_ext, (((0,), (0,)), ((), ())), preferred_element_type=jnp.float32)
    state_scr[...] = decay * state + upd
    m_scr[...] = jnp.broadcast_to(m_new, m_scr.shape)

    hn = h * lax.rsqrt(jnp.mean(h * h, axis=-1, keepdims=True) + EPS) * hg_ref[...]
    o_gate = 0.5 + 0.5 * jnp.tanh(0.5 * o_ref[...].astype(jnp.float32))
    y_ref[...] = (o_gate * hn).astype(y_ref.dtype)


def _mlstm_kernel(q_ref, k_ref, v_ref, o_ref, gate_ref, gbias_ref, cwq_ref, cwk_ref, cbq_ref, cbk_ref,
                  shift_ref, hg_ref, y_ref, state_scr, m_scr, pq_scr, pk_scr):
    dh = M_HEAD_DIM
    for hh in range(M_HEADS_PER_STEP):
        cols = pl.ds(hh * dh, dh)
        _mlstm_head(pl.program_id(0) * M_HEADS_PER_STEP + hh, pl.program_id(1),
                    q_ref.at[:, cols], k_ref.at[:, cols], v_ref.at[:, cols], o_ref.at[:, cols],
                    gate_ref, gbias_ref, cwq_ref.at[:, cols], cwk_ref.at[:, cols],
                    cbq_ref.at[:, cols], cbk_ref.at[:, cols], shift_ref, hg_ref, y_ref.at[:, cols],
                    state_scr.at[hh], m_scr.at[hh], pq_scr.at[hh], pk_scr.at[hh])


def _mlstm(proj, gates_t, gate_bias, conv_w, conv_b, mhn_g):
    s = proj.shape[0]
    L = min(M_CHUNK, s)
    nc = s // L
    dh = M_HEAD_DIM
    hps = M_HEADS_PER_STEP
    w = hps * dh
    ng = M_HEADS // hps
    base = D_MODEL // w
    taps = jnp.arange((CONV_WIDTH - 1) * L)
    shift = (jnp.arange(L)[None, :] == (taps % L - taps // L - 1)[:, None]).astype(jnp.bfloat16)
    return pl.pallas_call(
        _mlstm_kernel,
        grid=(ng, nc),
        in_specs=[
            pl.BlockSpec((L, w), lambda g, c: (c, base + g)),
            pl.BlockSpec((L, w), lambda g, c: (c, base + ng + g)),
            pl.BlockSpec((L, w), lambda g, c: (c, base + 2 * ng + g)),
            pl.BlockSpec((L, w), lambda g, c: (c, base + 3 * ng + g)),
            pl.BlockSpec((2 * M_HEADS, L), lambda g, c: (0, c)),
            pl.BlockSpec((2 * M_HEADS, 1), lambda g, c: (0, 0)),
            pl.BlockSpec((CONV_WIDTH, w), lambda g, c: (0, g)),
            pl.BlockSpec((CONV_WIDTH, w), lambda g, c: (0, ng + g)),
            pl.BlockSpec((1, w), lambda g, c: (0, g)),
            pl.BlockSpec((1, w), lambda g, c: (0, ng + g)),
            pl.BlockSpec(((CONV_WIDTH - 1) * L, L), lambda g, c: (0, 0)),
            pl.BlockSpec((1, dh), lambda g, c: (0, 0)),
        ],
        out_specs=pl.BlockSpec((L, w), lambda g, c: (c, g)),
        out_shape=jax.ShapeDtypeStruct((s, D_MODEL), jnp.bfloat16),
        scratch_shapes=[
            pltpu.VMEM((hps, dh, STATE_W), jnp.float32),
            pltpu.VMEM((hps, 8, LANES), jnp.float32),
            pltpu.VMEM((hps, 8, dh), jnp.float32),
            pltpu.VMEM((hps, 8, dh), jnp.float32),
        ],
        compiler_params=pltpu.CompilerParams(
            dimension_semantics=("arbitrary", "arbitrary"), vmem_limit_bytes=VMEM_LIMIT),
        name="mlstm",
    )(proj, proj, proj, proj, gates_t, gate_bias, conv_w, conv_w, conv_b, conv_b, shift, mhn_g)


def _merge_kernel(x_ref, ya_ref, ym_ref, ga_ref, gm_ref, wa_ref, wm_ref, wo_ref, gt_ref, o_ref):
    a = _f32dot(ya_ref[...], wa_ref[...])
    b = _f32dot(ym_ref[...], wm_ref[...])
    merged = (jax.nn.sigmoid(ga_ref[...].astype(jnp.float32)) * a
              + jax.nn.sigmoid(gm_ref[...].astype(jnp.float32)) * b)
    o_ref[...] = x_ref[...] + gt_ref[...] * _f32dot(merged.astype(jnp.bfloat16), wo_ref[...])


def _merge(x, ya, ym, proj, w_a, w_m, w_out, gt):
    s = x.shape[0]
    tm = min(MLP_TILE, s)
    row = lambda i: (i, 0)
    const = lambda i: (0, 0)
    wspec = pl.BlockSpec((D_MODEL, D_MODEL), const, pipeline_mode=pl.Buffered(1))
    return pl.pallas_call(
        _merge_kernel,
        grid=(s // tm,),
        in_specs=[
            pl.BlockSpec((tm, D_MODEL), row),
            pl.BlockSpec((tm, D_MODEL), row),
            pl.BlockSpec((tm, D_MODEL), row),
            pl.BlockSpec((tm, D_MODEL), lambda i: (i, 5)),
            pl.BlockSpec((tm, D_MODEL), lambda i: (i, 6)),
            wspec, wspec, wspec,
            pl.BlockSpec((1, D_MODEL), const),
        ],
        out_specs=pl.BlockSpec((tm, D_MODEL), row),
        out_shape=jax.ShapeDtypeStruct((s, D_MODEL), jnp.float32),
        compiler_params=pltpu.CompilerParams(
            dimension_semantics=("arbitrary",), vmem_limit_bytes=VMEM_LIMIT),
        name="merge_out_projection",
    )(x, ya, ym, proj, proj, w_a, w_m, w_out, gt)


def _mlp_kernel(x_ref, g_ref, sc_ref, sh_ref, gt_ref, w1_ref, w2_ref, o_ref):
    x = x_ref[...]
    hb = _modulated_norm(x, g_ref[...], sc_ref[...], sh_ref[...]).astype(jnp.bfloat16)
    acc = None
    for t in range(D_FF // D_MODEL):
        u = jnp.maximum(_f32dot(hb, w1_ref[:, t * D_MODEL:(t + 1) * D_MODEL]), 0.0)
        part = _f32dot((u * u).astype(jnp.bfloat16), w2_ref[t * D_MODEL:(t + 1) * D_MODEL, :])
        acc = part if acc is None else acc + part
    o_ref[...] = x + gt_ref[...] * acc


def _mlp(x, g, sc, sh, gt, w1, w2):
    s = x.shape[0]
    tm = min(MLP_TILE, s)
    row = lambda i: (i, 0)
    const = lambda i: (0, 0)
    vec = pl.BlockSpec((1, D_MODEL), const)
    return pl.pallas_call(
        _mlp_kernel,
        grid=(s // tm,),
        in_specs=[
            pl.BlockSpec((tm, D_MODEL), row),
            vec, vec, vec, vec,
            pl.BlockSpec((D_MODEL, D_FF), const, pipeline_mode=pl.Buffered(1)),
            pl.BlockSpec((D_FF, D_MODEL), const, pipeline_mode=pl.Buffered(1)),
        ],
        out_specs=pl.BlockSpec((tm, D_MODEL), row),
        out_shape=jax.ShapeDtypeStruct((s, D_MODEL), jnp.float32),
        compiler_params=pltpu.CompilerParams(
            dimension_semantics=("arbitrary",), vmem_limit_bytes=VMEM_LIMIT),
        name="relu2_mlp",
    )(x, g, sc, sh, gt, w1, w2)


def kernel(x, c, w_ada, b_ada, norm_mix_g, norm_ffn_g, w_in, b_igate, b_fgate, qn_g, kn_g, lam_qk,
           subln_g, rel_table, conv_w, conv_b, mhn_g, w_a, w_m, w_out, w_ff1, w_ff2):
    b, s, d = x.shape
    assert b == 1 and d == D_MODEL
    depth = w_ada.shape[0]
    bf = jnp.bfloat16
    xs = x[0]

    mod = _ada_modulation(c, w_ada, b_ada)
    bias_tiles = _attention_bias_tiles(rel_table, s)
    blk = A_HEAD_DIM
    bd = (jnp.arange(256)[:, None] // blk == jnp.arange(256)[None, :] // blk).astype(bf)

    gate_lo = 7 * D_MODEL
    gate_hi = gate_lo + 2 * M_HEADS
    for l in range(depth):
        lam_init = 0.8 - 0.6 * math.exp(-0.3 * l)
        sh_a, sc_a, gt_a, sh_f, sc_f, gt_f = [mod[l:l + 1, t * d:(t + 1) * d] for t in range(6)]

        w = w_in[l]
        w_main = jnp.concatenate([w[:, :gate_lo], w[:, gate_hi:]], axis=1).astype(bf)
        w_gate = jnp.pad(w[:, gate_lo:gate_hi], ((0, 0), (0, GATE_PAD - 2 * M_HEADS))).astype(bf)
        reps = A_WIDTH // A_HEAD_DIM
        qk_gain = jnp.stack([jnp.tile(qn_g[l], reps) * (A_HEAD_DIM ** -0.5 * LOG2E),
                             jnp.tile(kn_g[l], reps)])[:, None, :]

        proj, gates, qzt, vt = _in_projection(xs, norm_mix_g[l][None], sc_a, sh_a, w_main, w_gate,
                                              qk_gain, bd)

        ya = _diff_attention(qzt, proj, vt, bias_tiles, lam_qk[l], subln_g[l][:, None], lam_init)

        gates_t = gates[:, :2 * M_HEADS].T
        gate_bias = jnp.concatenate([b_igate[l], b_fgate[l]])[:, None]
        ym = _mlstm(proj, gates_t, gate_bias, conv_w[l], conv_b[l][None], mhn_g[l][None])

        xs = _merge(xs, ya, ym, proj, w_a[l].astype(bf), w_m[l].astype(bf), w_out[l].astype(bf), gt_a)
        xs = _mlp(xs, norm_ffn_g[l][None], sc_f, sh_f, gt_f, w_ff1[l].astype(bf), w_ff2[l].astype(bf))
    return xs[None]
```

```python
import functools
import math

import jax
import jax.numpy as jnp
from jax import lax
from jax.experimental import pallas as pl
from jax.experimental.pallas import tpu as pltpu

D_MODEL = 1024
A_HEADS = 8
A_HEAD_DIM = 64
A_WIDTH = A_HEADS * 2 * A_HEAD_DIM
M_HEADS = 4
M_HEAD_DIM = D_MODEL // M_HEADS
CONV_WIDTH = 4
NUM_BUCKETS = 32
MAX_DISTANCE = 128
D_FF = 4 * D_MODEL
EPS = 1e-6

LANES = 128
GATE_PAD = LANES
N_ROWMAJOR = 7 * D_MODEL
VMEM_LIMIT = 56 * 1024 * 1024
NEG_BIG = -1e30
LOG2E = math.log2(math.e)

ATT_BLOCK = 512
M_CHUNK = 256
M_HEADS_PER_STEP = 2
MLP_TILE = 512
STATE_W = M_HEAD_DIM + LANES


def _f32dot(a, b):
    return jnp.dot(a, b, preferred_element_type=jnp.float32)


def _modulated_norm(x, g, sc, sh):
    y = x * lax.rsqrt(jnp.mean(x * x, axis=-1, keepdims=True) + EPS)
    return (y * g) * (1.0 + sc) + sh


def _ada_kernel(c_ref, w_ref, b_ref, o_ref):
    c = c_ref[...]
    cond = c * jax.nn.sigmoid(c)
    o_ref[...] = jnp.sum(cond * w_ref[...], axis=0, keepdims=True) + b_ref[...]


def _ada_modulation(c, w_ada, b_ada):
    depth = w_ada.shape[0]
    nblk = w_ada.shape[2] // D_MODEL
    out = pl.pallas_call(
        _ada_kernel,
        grid=(depth, nblk),
        in_specs=[
            pl.BlockSpec((D_MODEL, 1), lambda l, j: (0, 0)),
            pl.BlockSpec((None, D_MODEL, D_MODEL), lambda l, j: (l, 0, j)),
            pl.BlockSpec((None, 1, D_MODEL), lambda l, j: (l, 0, j)),
        ],
        out_specs=pl.BlockSpec((None, 1, D_MODEL), lambda l, j: (l, 0, j)),
        out_shape=jax.ShapeDtypeStruct((depth, 1, w_ada.shape[2]), jnp.float32),
        name="ada_modulation",
    )(c.reshape(D_MODEL, 1), w_ada, b_ada[:, None, :])
    return out[:, 0, :]


def _proj_kernel(x_ref, g_ref, sc_ref, sh_ref, w_ref, wg_ref, qkg_ref, bd_ref,
                 o_ref, gate_ref, qzt_ref, vt_ref):
    tm = x_ref.shape[0]
    dh = A_HEAD_DIM
    h = _modulated_norm(x_ref[...], g_ref[...], sc_ref[...], sh_ref[...])
    hb = h.astype(jnp.bfloat16)
    gate_ref[...] = _f32dot(hb, wg_ref[...])
    bd = bd_ref[...]
    lane = lax.broadcasted_iota(jnp.int32, (tm, LANES), 1)
    for j in range(w_ref.shape[1] // D_MODEL):
        y = _f32dot(hb, w_ref[:, j * D_MODEL:(j + 1) * D_MODEL])
        if j == 2:
            for hd in range(A_HEADS):
                vt_ref[hd, 0] = y[:, hd * LANES:(hd + 1) * LANES].T.astype(vt_ref.dtype)
        elif j > 2:
            o_ref[:, (j - 2) * D_MODEL:(j - 1) * D_MODEL] = y.astype(o_ref.dtype)
        else:
            for t in range(D_MODEL // 256):
                yt = y[:, t * 256:(t + 1) * 256]
                ss = _f32dot((yt * yt).astype(jnp.bfloat16), bd)
                yn = yt * lax.rsqrt(ss * (1.0 / A_HEAD_DIM) + EPS) * qkg_ref[j, :, t * 256:(t + 1) * 256]
                if j == 1:
                    o_ref[:, t * 256:(t + 1) * 256] = yn.astype(o_ref.dtype)
                    continue
                for hh in range(256 // LANES):
                    hd = t * (256 // LANES) + hh
                    qh = yn[:, hh * LANES:(hh + 1) * LANES]
                    qzt_ref[hd, 0, :, :tm] = jnp.where(lane < dh, qh, 0.0).T.astype(qzt_ref.dtype)
                    qzt_ref[hd, 0, :, tm:] = jnp.where(lane >= dh, qh, 0.0).T.astype(qzt_ref.dtype)


def _in_projection(x, g, sc, sh, w_main, w_gate, qk_gain, bd):
    s = x.shape[0]
    tm = min(ATT_BLOCK, s)
    nt = s // tm
    const = lambda i: (0, 0)
    vec = pl.BlockSpec((1, D_MODEL), const)
    resident = pl.Buffered(1)
    return pl.pallas_call(
        _proj_kernel,
        grid=(nt,),
        in_specs=[
            pl.BlockSpec((tm, D_MODEL), lambda i: (i, 0)),
            vec, vec, vec,
            pl.BlockSpec((D_MODEL, w_main.shape[1]), const, pipeline_mode=resident),
            pl.BlockSpec((D_MODEL, GATE_PAD), const, pipeline_mode=resident),
            pl.BlockSpec((2, 1, D_MODEL), lambda i: (0, 0, 0)),
            pl.BlockSpec((256, 256), const),
        ],
        out_specs=[
            pl.BlockSpec((tm, N_ROWMAJOR), lambda i: (i, 0)),
            pl.BlockSpec((tm, GATE_PAD), lambda i: (i, 0)),
            pl.BlockSpec((A_HEADS, 1, LANES, 2 * tm), lambda i: (0, i, 0, 0)),
            pl.BlockSpec((A_HEADS, 1, LANES, tm), lambda i: (0, i, 0, 0)),
        ],
        out_shape=[
            jax.ShapeDtypeStruct((s, N_ROWMAJOR), jnp.bfloat16),
            jax.ShapeDtypeStruct((s, GATE_PAD), jnp.float32),
            jax.ShapeDtypeStruct((A_HEADS, nt, LANES, 2 * tm), jnp.bfloat16),
            jax.ShapeDtypeStruct((A_HEADS, nt, LANES, tm), jnp.bfloat16),
        ],
        compiler_params=pltpu.CompilerParams(
            dimension_semantics=("arbitrary",), vmem_limit_bytes=VMEM_LIMIT),
        name="in_projection",
    )(x, g, sc, sh, w_main, w_gate, qk_gain, bd)


def _attn_kernel(lam_init, qzt_ref, k_ref, vt_ref, ones_ref, bias_ref, lamqk_ref, g_ref, o_ref,
                 sa_scr, sb_scr, mxa_scr, mxb_scr, m_scr, acc_scr):
    i = pl.program_id(1)
    bq = qzt_ref.shape[1] // 2
    bk = bq
    dv = 2 * A_HEAD_DIM

    m_scr[...] = jnp.full(m_scr.shape, NEG_BIG, jnp.float32)
    acc_scr[...] = jnp.zeros(acc_scr.shape, jnp.float32)

    def logits(j, s_ref, mx_ref):
        kb = k_ref[pl.ds(pl.multiple_of(j * bk, bk), bk), :]
        st = _f32dot(kb, qzt_ref[...])
        s_ref[:, :2 * bq] = st
        mx_ref[...] = jnp.max(st, axis=0, keepdims=True)

    def consume(j, s_ref, mx_ref, bias_idx):
        st = s_ref[:, :2 * bq]
        if bias_idx is None:
            mx = mx_ref[...]
        else:
            tile = bias_ref[bias_idx]
            st = st + jnp.concatenate([tile, tile], axis=1)
            mx = jnp.max(st, axis=0, keepdims=True)
        m_old = m_scr[...]
        m_new = jnp.maximum(m_old, mx)
        pt = jnp.exp2(st - m_new).astype(jnp.bfloat16)
        vt = jnp.concatenate([vt_ref[j], ones_ref[...]], axis=0)
        pv = _f32dot(vt, pt)
        acc_scr[...] = acc_scr[...] * jnp.exp2(m_old - m_new) + pv
        m_scr[...] = m_new

    NEAR, DIAG = 1, 0
    n_far = jnp.maximum(i - 1, 0)
    buf_a = (sa_scr, mxa_scr)
    buf_b = (sb_scr, mxb_scr)
    logits(0, *buf_a)

    def far_pair(j):
        logits(j + 1, *buf_b)
        consume(j, *buf_a, None)
        logits(j + 2, *buf_a)
        consume(j + 1, *buf_b, None)

    def far_octet(t, carry):
        for pair in range(4):
            far_pair(8 * t + 2 * pair)
        return carry

    lax.fori_loop(0, n_far // 8, far_octet, 0)

    @pl.when(n_far % 8 >= 4)
    def _():
        far_pair(8 * (n_far // 8))
        far_pair(8 * (n_far // 8) + 2)

    @pl.when(n_far % 4 >= 2)
    def _():
        far_pair(4 * (n_far // 4))

    @pl.when(i == 0)
    def _():
        consume(i, *buf_a, DIAG)

    @pl.when(jnp.logical_and(i > 0, n_far % 2 == 0))
    def _():
        logits(i, *buf_b)
        consume(i - 1, *buf_a, NEAR)
        consume(i, *buf_b, DIAG)

    @pl.when(n_far % 2 == 1)
    def _():
        logits(i - 1, *buf_b)
        consume(i - 2, *buf_a, None)
        logits(i, *buf_a)
        consume(i - 1, *buf_b, NEAR)
        consume(i, *buf_a, DIAG)

    lq = lamqk_ref[...]
    e1 = jnp.exp(jnp.sum(lq[0:1] * lq[1:2], axis=1, keepdims=True))
    e2 = jnp.exp(jnp.sum(lq[2:3] * lq[3:4], axis=1, keepdims=True))
    lam = e1 - e2 + lam_init

    o = acc_scr[:dv, :] * (1.0 / acc_scr[dv:dv + 1, :])
    ot = o[:, :bq] - lam * o[:, bq:]
    ms = jnp.mean(ot * ot, axis=0, keepdims=True)
    on = ot * lax.rsqrt(ms + EPS) * g_ref[...] * (1.0 - lam_init)
    o_ref[...] = on.T.astype(o_ref.dtype)


def _diff_attention(qzt, proj, vt, bias_tiles, lam_qk, subln_col, lam_init):
    s = proj.shape[0]
    bq = min(ATT_BLOCK, s)
    nq = s // bq
    pad_rows = 16
    ones_rows = jnp.zeros((pad_rows, bq), jnp.bfloat16).at[0].set(1.0)
    return pl.pallas_call(
        functools.partial(_attn_kernel, lam_init),
        grid=(A_HEADS, nq),
        in_specs=[
            pl.BlockSpec((None, None, LANES, 2 * bq), lambda h, i: (h, i, 0, 0)),
            pl.BlockSpec((s, LANES), lambda h, i: (0, h)),
            pl.BlockSpec((None, nq, LANES, bq), lambda h, i: (h, 0, 0, 0)),
            pl.BlockSpec((pad_rows, bq), lambda h, i: (0, 0)),
            pl.BlockSpec((None, 2, bq, bq), lambda h, i: (h, 0, 0, 0)),
            pl.BlockSpec((4, A_HEAD_DIM), lambda h, i: (0, 0)),
            pl.BlockSpec((LANES, 1), lambda h, i: (0, 0)),
        ],
        out_specs=pl.BlockSpec((bq, LANES), lambda h, i: (i, h)),
        out_shape=jax.ShapeDtypeStruct((s, A_WIDTH), jnp.bfloat16),
        scratch_shapes=[
            pltpu.VMEM((bq, 2 * bq + LANES), jnp.float32),
            pltpu.VMEM((bq, 2 * bq + LANES), jnp.float32),
            pltpu.VMEM((1, 2 * bq), jnp.float32),
            pltpu.VMEM((1, 2 * bq), jnp.float32),
            pltpu.VMEM((1, 2 * bq), jnp.float32),
            pltpu.VMEM((LANES + pad_rows, 2 * bq), jnp.float32),
        ],
        compiler_params=pltpu.CompilerParams(
            dimension_semantics=("arbitrary", "arbitrary"), vmem_limit_bytes=VMEM_LIMIT),
        name="diff_attention",
    )(qzt, proj, vt, ones_rows, bias_tiles, lam_qk, subln_col)


def _t5_bucket(dist):
    max_exact = NUM_BUCKETS // 2
    nf = jnp.maximum(dist, 1).astype(jnp.float32)
    large = max_exact + (jnp.log(nf / max_exact) / math.log(MAX_DISTANCE / max_exact)
                         * (NUM_BUCKETS - max_exact)).astype(jnp.int32)
    large = jnp.minimum(large, NUM_BUCKETS - 1)
    return jnp.where(dist < max_exact, dist, large)


def _attention_bias_tiles(rel_table, s):
    bq = min(ATT_BLOCK, s)
    dists = jnp.arange(2 * bq)
    bias = rel_table[_t5_bucket(dists)].T.astype(jnp.float32)
    far = rel_table[_t5_bucket(jnp.array([s - 1]))].T.astype(jnp.float32)
    bias = (bias - far) * LOG2E
    period = 2 * bq + 1
    masked = jnp.full((A_HEADS, bq + 1), NEG_BIG, jnp.float32)
    u_diag = jnp.concatenate([bias[:, :bq], masked], axis=1)
    u_near = jnp.concatenate([bias[:, bq:], jnp.zeros((A_HEADS, 1), jnp.float32), bias[:, :bq]], axis=1)
    u = jnp.stack([u_diag, u_near], axis=1)
    flat = jnp.tile(u, (1, 1, bq))[:, :, :bq * 2 * bq]
    return flat.reshape(A_HEADS, 2, bq, 2 * bq)[..., :bq]


def _mlstm_head(hd, c, q_ref, k_ref, v_ref, o_ref, gate_ref, gbias_ref, cwq_ref, cwk_ref, cbq_ref, cbk_ref,
                shift_ref, hg_ref, y_ref, state_scr, m_scr, pq_scr, pk_scr):
    L = q_ref.shape[0]
    dh = M_HEAD_DIM

    @pl.when(c == 0)
    def _():
        for scr in (state_scr, m_scr, pq_scr, pk_scr):
            scr[...] = jnp.zeros(scr.shape, jnp.float32)

    row8 = lax.broadcasted_iota(jnp.int32, (8, dh), 0)

    def conv_silu(u_ref, tail_scr, w_ref, b_ref):
        ub = u_ref[...]
        u = ub.astype(jnp.float32)
        w = w_ref[...]
        taps = _f32dot(shift_ref[...], ub)
        tail = tail_scr[...]
        y = b_ref[...] + w[CONV_WIDTH - 1:CONV_WIDTH] * u
        top = jnp.zeros((8, dh), jnp.float32)
        for back in range(1, CONV_WIDTH):
            wb = w[CONV_WIDTH - 1 - back:CONV_WIDTH - back]
            y = y + wb * taps[(back - 1) * L:back * L]
            top = top + wb * jnp.where(row8 < back, pltpu.roll(tail, back, 0), 0.0)
        y = jnp.concatenate([y[:8] + top, y[8:]], axis=0)
        tail_scr[...] = u[L - 8:]
        half = 0.5 * y
        return half + half * jnp.tanh(half)

    qc = conv_silu(q_ref, pq_scr, cwq_ref, cbq_ref)
    kc = conv_silu(k_ref, pk_scr, cwk_ref, cbk_ref) * (dh ** -0.5)
    qb = qc.astype(jnp.bfloat16)
    kb = kc.astype(jnp.bfloat16)

    gb = gbias_ref[...]
    gates = gate_ref[...] + gb
    sub = lax.broadcasted_iota(jnp.int32, gates.shape, 0)
    ig_row = jnp.sum(jnp.where(sub == hd, gates, 0.0), axis=0, keepdims=True)
    fg_row = jnp.sum(jnp.where(sub == hd + M_HEADS, gates, 0.0), axis=0, keepdims=True)
    lf_row = jax.nn.log_sigmoid(fg_row)

    t_idx = lax.broadcasted_iota(jnp.int32, (L, L), 0)
    s_idx = lax.broadcasted_iota(jnp.int32, (L, L), 1)
    tril = s_idx <= t_idx
    eye = s_idx == t_idx

    def to_col(r):
        return jnp.sum(jnp.where(eye, r, 0.0), axis=1, keepdims=True)

    def to_row(col):
        return jnp.sum(jnp.where(eye, col, 0.0), axis=0, keepdims=True)

    bcum_col = jnp.sum(jnp.where(tril, lf_row, 0.0), axis=1, keepdims=True)
    bcum_row = to_row(bcum_col)
    ig_col = to_col(ig_row)
    b_last = bcum_col[L - 1:L, :]
    m_prev = m_scr[0:1, 0:1]

    dmat = jnp.where(tril, bcum_col - bcum_row + ig_row, NEG_BIG)
    inter = bcum_col + m_prev
    m_t = jnp.maximum(jnp.max(dmat, axis=1, keepdims=True), inter)
    qk = lax.dot_general(qb, kb, (((1,), (1,)), ((), ())), preferred_element_type=jnp.float32)
    scores = qk * jnp.exp(dmat - m_t)
    a_inter = jnp.exp(inter - m_t)

    lane_w = lax.broadcasted_iota(jnp.int32, (L, LANES), 1)
    v_ext = jnp.concatenate(
        [v_ref[...], jnp.where(lane_w == 0, 1.0, 0.0).astype(jnp.bfloat16)], axis=1)
    state = state_scr[...]
    res = _f32dot(scores.astype(jnp.bfloat16), v_ext) + a_inter * _f32dot(qb, state.astype(jnp.bfloat16))
    num = res[:, :dh]
    den = res[:, dh:dh + 1]
    h = num / jnp.maximum(jnp.abs(den), jnp.exp(-m_t))

    g_col = b_last - bcum_col + ig_col
    m_new = jnp.maximum(b_last + m_prev, jnp.max(g_col, axis=0, keepdims=True))
    wk = jnp.exp(g_col - m_new)
    decay = jnp.exp(b_last + m_prev - m_new)
    kw = (kc * wk).astype(jnp.bfloat16)
    upd = lax.dot_general(kw, v_ext, (((0,), (0,)), ((), ())), preferred_element_type=jnp.float32)
    state_scr[...] = decay * state + upd
    m_scr[...] = jnp.broadcast_to(m_new, m_scr.shape)

    hn = h * lax.rsqrt(jnp.mean(h * h, axis=-1, keepdims=True) + EPS) * hg_ref[...]
    o_gate = 0.5 + 0.5 * jnp.tanh(0.5 * o_ref[...].astype(jnp.float32))
    y_ref[...] = (o_gate * hn).astype(y_ref.dtype)


def _mlstm_kernel(q_ref, k_ref, v_ref, o_ref, gate_ref, gbias_ref, cwq_ref, cwk_ref, cbq_ref, cbk_ref,
                  shift_ref, hg_ref, y_ref, state_scr, m_scr, pq_scr, pk_scr):
    dh = M_HEAD_DIM
    for hh in range(M_HEADS_PER_STEP):
        cols = pl.ds(hh * dh, dh)
        _mlstm_head(pl.program_id(0) * M_HEADS_PER_STEP + hh, pl.program_id(1),
                    q_ref.at[:, cols], k_ref.at[:, cols], v_ref.at[:, cols], o_ref.at[:, cols],
                    gate_ref, gbias_ref, cwq_ref.at[:, cols], cwk_ref.at[:, cols],
                    cbq_ref.at[:, cols], cbk_ref.at[:, cols], shift_ref, hg_ref, y_ref.at[:, cols],
                    state_scr.at[hh], m_scr.at[hh], pq_scr.at[hh], pk_scr.at[hh])


def _mlstm(proj, gates_t, gate_bias, conv_w, conv_b, mhn_g):
    s = proj.shape[0]
    L = min(M_CHUNK, s)
    nc = s // L
    dh = M_HEAD_DIM
    hps = M_HEADS_PER_STEP
    w = hps * dh
    ng = M_HEADS // hps
    base = D_MODEL // w
    taps = jnp.arange((CONV_WIDTH - 1) * L)
    shift = (jnp.arange(L)[None, :] == (taps % L - taps // L - 1)[:, None]).astype(jnp.bfloat16)
    return pl.pallas_call(
        _mlstm_kernel,
        grid=(ng, nc),
        in_specs=[
            pl.BlockSpec((L, w), lambda g, c: (c, base + g)),
            pl.BlockSpec((L, w), lambda g, c: (c, base + ng + g)),
            pl.BlockSpec((L, w), lambda g, c: (c, base + 2 * ng + g)),
            pl.BlockSpec((L, w), lambda g, c: (c, base + 3 * ng + g)),
            pl.BlockSpec((2 * M_HEADS, L), lambda g, c: (0, c)),
            pl.BlockSpec((2 * M_HEADS, 1), lambda g, c: (0, 0)),
            pl.BlockSpec((CONV_WIDTH, w), lambda g, c: (0, g)),
            pl.BlockSpec((CONV_WIDTH, w), lambda g, c: (0, ng + g)),
            pl.BlockSpec((1, w), lambda g, c: (0, g)),
            pl.BlockSpec((1, w), lambda g, c: (0, ng + g)),
            pl.BlockSpec(((CONV_WIDTH - 1) * L, L), lambda g, c: (0, 0)),
            pl.BlockSpec((1, dh), lambda g, c: (0, 0)),
        ],
        out_specs=pl.BlockSpec((L, w), lambda g, c: (c, g)),
        out_shape=jax.ShapeDtypeStruct((s, D_MODEL), jnp.bfloat16),
        scratch_shapes=[
            pltpu.VMEM((hps, dh, STATE_W), jnp.float32),
            pltpu.VMEM((hps, 8, LANES), jnp.float32),
            pltpu.VMEM((hps, 8, dh), jnp.float32),
            pltpu.VMEM((hps, 8, dh), jnp.float32),
        ],
        compiler_params=pltpu.CompilerParams(
            dimension_semantics=("arbitrary", "arbitrary"), vmem_limit_bytes=VMEM_LIMIT),
        name="mlstm",
    )(proj, proj, proj, proj, gates_t, gate_bias, conv_w, conv_w, conv_b, conv_b, shift, mhn_g)


def _merge_kernel(x_ref, ya_ref, ym_ref, ga_ref, gm_ref, wa_ref, wm_ref, wo_ref, gt_ref, o_ref):
    a = _f32dot(ya_ref[...], wa_ref[...])
    b = _f32dot(ym_ref[...], wm_ref[...])
    merged = (jax.nn.sigmoid(ga_ref[...].astype(jnp.float32)) * a
              + jax.nn.sigmoid(gm_ref[...].astype(jnp.float32)) * b)
    o_ref[...] = x_ref[...] + gt_ref[...] * _f32dot(merged.astype(jnp.bfloat16), wo_ref[...])


def _merge(x, ya, ym, proj, w_a, w_m, w_out, gt):
    s = x.shape[0]
    tm = min(MLP_TILE, s)
    row = lambda i: (i, 0)
    const = lambda i: (0, 0)
    wspec = pl.BlockSpec((D_MODEL, D_MODEL), const, pipeline_mode=pl.Buffered(1))
    return pl.pallas_call(
        _merge_kernel,
        grid=(s // tm,),
        in_specs=[
            pl.BlockSpec((tm, D_MODEL), row),
            pl.BlockSpec((tm, D_MODEL), row),
            pl.BlockSpec((tm, D_MODEL), row),
            pl.BlockSpec((tm, D_MODEL), lambda i: (i, 5)),
            pl.BlockSpec((tm, D_MODEL), lambda i: (i, 6)),
            wspec, wspec, wspec,
            pl.BlockSpec((1, D_MODEL), const),
        ],
        out_specs=pl.BlockSpec((tm, D_MODEL), row),
        out_shape=jax.ShapeDtypeStruct((s, D_MODEL), jnp.float32),
        compiler_params=pltpu.CompilerParams(
            dimension_semantics=("arbitrary",), vmem_limit_bytes=VMEM_LIMIT),
        name="merge_out_projection",
    )(x, ya, ym, proj, proj, w_a, w_m, w_out, gt)


def _mlp_kernel(x_ref, g_ref, sc_ref, sh_ref, gt_ref, w1_ref, w2_ref, o_ref):
    x = x_ref[...]
    hb = _modulated_norm(x, g_ref[...], sc_ref[...], sh_ref[...]).astype(jnp.bfloat16)
    acc = None
    for t in range(D_FF // D_MODEL):
        u = jnp.maximum(_f32dot(hb, w1_ref[:, t * D_MODEL:(t + 1) * D_MODEL]), 0.0)
        part = _f32dot((u * u).astype(jnp.bfloat16), w2_ref[t * D_MODEL:(t + 1) * D_MODEL, :])
        acc = part if acc is None else acc + part
    o_ref[...] = x + gt_ref[...] * acc


def _mlp(x, g, sc, sh, gt, w1, w2):
    s = x.shape[0]
    tm = min(MLP_TILE, s)
    row = lambda i: (i, 0)
    const = lambda i: (0, 0)
    vec = pl.BlockSpec((1, D_MODEL), const)
    return pl.pallas_call(
        _mlp_kernel,
        grid=(s // tm,),
        in_specs=[
            pl.BlockSpec((tm, D_MODEL), row),
            vec, vec, vec, vec,
            pl.BlockSpec((D_MODEL, D_FF), const, pipeline_mode=pl.Buffered(1)),
            pl.BlockSpec((D_FF, D_MODEL), const, pipeline_mode=pl.Buffered(1)),
        ],
        out_specs=pl.BlockSpec((tm, D_MODEL), row),
        out_shape=jax.ShapeDtypeStruct((s, D_MODEL), jnp.float32),
        compiler_params=pltpu.CompilerParams(
            dimension_semantics=("arbitrary",), vmem_limit_bytes=VMEM_LIMIT),
        name="relu2_mlp",
    )(x, g, sc, sh, gt, w1, w2)


def kernel(x, c, w_ada, b_ada, norm_mix_g, norm_ffn_g, w_in, b_igate, b_fgate, qn_g, kn_g, lam_qk,
           subln_g, rel_table, conv_w, conv_b, mhn_g, w_a, w_m, w_out, w_ff1, w_ff2):
    b, s, d = x.shape
    assert b == 1 and d == D_MODEL
    depth = w_ada.shape[0]
    bf = jnp.bfloat16
    xs = x[0]

    mod = _ada_modulation(c, w_ada, b_ada)
    bias_tiles = _attention_bias_tiles(rel_table, s)
    blk = A_HEAD_DIM
    bd = (jnp.arange(256)[:, None] // blk == jnp.arange(256)[None, :] // blk).astype(bf)

    gate_lo = 7 * D_MODEL
    gate_hi = gate_lo + 2 * M_HEADS
    for l in range(depth):
        lam_init = 0.8 - 0.6 * math.exp(-0.3 * l)
        sh_a, sc_a, gt_a, sh_f, sc_f, gt_f = [mod[l:l + 1, t * d:(t + 1) * d] for t in range(6)]

        w = w_in[l]
        w_main = jnp.concatenate([w[:, :gate_lo], w[:, gate_hi:]], axis=1).astype(bf)
        w_gate = jnp.pad(w[:, gate_lo:gate_hi], ((0, 0), (0, GATE_PAD - 2 * M_HEADS))).astype(bf)
        reps = A_WIDTH // A_HEAD_DIM
        qk_gain = jnp.stack([jnp.tile(qn_g[l], reps) * (A_HEAD_DIM ** -0.5 * LOG2E),
                             jnp.tile(kn_g[l], reps)])[:, None, :]

        proj, gates, qzt, vt = _in_projection(xs, norm_mix_g[l][None], sc_a, sh_a, w_main, w_gate,
                                              qk_gain, bd)

        ya = _diff_attention(qzt, proj, vt, bias_tiles, lam_qk[l], subln_g[l][:, None], lam_init)

        gates_t = gates[:, :2 * M_HEADS].T
        gate_bias = jnp.concatenate([b_igate[l], b_fgate[l]])[:, None]
        ym = _mlstm(proj, gates_t, gate_bias, conv_w[l], conv_b[l][None], mhn_g[l][None])

        xs = _merge(xs, ya, ym, proj, w_a[l].astype(bf), w_m[l].astype(bf), w_out[l].astype(bf), gt_a)
        xs = _mlp(xs, norm_ffn_g[l][None], sc_f, sh_f, gt_f, w_ff1[l].astype(bf), w_ff2[l].astype(bf))
    return xs[None]
```

```python
import functools
import math

import jax
import jax.numpy as jnp
from jax import lax
from jax.experimental import pallas as pl
from jax.experimental.pallas import tpu as pltpu

D_MODEL = 1024
A_HEADS = 8
A_HEAD_DIM = 64
A_WIDTH = A_HEADS * 2 * A_HEAD_DIM
M_HEADS = 4
M_HEAD_DIM = D_MODEL // M_HEADS
CONV_WIDTH = 4
NUM_BUCKETS = 32
MAX_DISTANCE = 128
D_FF = 4 * D_MODEL
EPS = 1e-6

LANES = 128
GATE_PAD = LANES
N_ROWMAJOR = 7 * D_MODEL
VMEM_LIMIT = 56 * 1024 * 1024
NEG_BIG = -1e30
LOG2E = math.log2(math.e)

ATT_BLOCK = 512
M_CHUNK = 256
M_HEADS_PER_STEP = 4
MLP_TILE = 512
STATE_W = M_HEAD_DIM + LANES


def _f32dot(a, b):
    return jnp.dot(a, b, preferred_element_type=jnp.float32)


def _modulated_norm(x, g, sc, sh):
    y = x * lax.rsqrt(jnp.mean(x * x, axis=-1, keepdims=True) + EPS)
    return (y * g) * (1.0 + sc) + sh


def _ada_kernel(c_ref, w_ref, b_ref, o_ref):
    c = c_ref[...]
    cond = c * jax.nn.sigmoid(c)
    o_ref[...] = jnp.sum(cond * w_ref[...], axis=0, keepdims=True) + b_ref[...]


def _ada_modulation(c, w_ada, b_ada):
    depth = w_ada.shape[0]
    nblk = w_ada.shape[2] // D_MODEL
    out = pl.pallas_call(
        _ada_kernel,
        grid=(depth, nblk),
        in_specs=[
            pl.BlockSpec((D_MODEL, 1), lambda l, j: (0, 0)),
            pl.BlockSpec((None, D_MODEL, D_MODEL), lambda l, j: (l, 0, j)),
            pl.BlockSpec((None, 1, D_MODEL), lambda l, j: (l, 0, j)),
        ],
        out_specs=pl.BlockSpec((None, 1, D_MODEL), lambda l, j: (l, 0, j)),
        out_shape=jax.ShapeDtypeStruct((depth, 1, w_ada.shape[2]), jnp.float32),
        name="ada_modulation",
    )(c.reshape(D_MODEL, 1), w_ada, b_ada[:, None, :])
    return out[:, 0, :]


def _proj_kernel(x_ref, g_ref, sc_ref, sh_ref, w_ref, wg_ref, qkg_ref, bd_ref,
                 o_ref, gate_ref, qzt_ref, vt_ref):
    tm = x_ref.shape[0]
    dh = A_HEAD_DIM
    h = _modulated_norm(x_ref[...], g_ref[...], sc_ref[...], sh_ref[...])
    hb = h.astype(jnp.bfloat16)
    gate_ref[...] = _f32dot(hb, wg_ref[...])
    bd = bd_ref[...]
    lane = lax.broadcasted_iota(jnp.int32, (tm, LANES), 1)
    for j in range(w_ref.shape[1] // D_MODEL):
        y = _f32dot(hb, w_ref[:, j * D_MODEL:(j + 1) * D_MODEL])
        if j == 2:
            for hd in range(A_HEADS):
                vt_ref[hd, 0] = y[:, hd * LANES:(hd + 1) * LANES].T.astype(vt_ref.dtype)
        elif j > 2:
            o_ref[:, (j - 2) * D_MODEL:(j - 1) * D_MODEL] = y.astype(o_ref.dtype)
        else:
            for t in range(D_MODEL // 256):
                yt = y[:, t * 256:(t + 1) * 256]
                ss = _f32dot((yt * yt).astype(jnp.bfloat16), bd)
                yn = yt * lax.rsqrt(ss * (1.0 / A_HEAD_DIM) + EPS) * qkg_ref[j, :, t * 256:(t + 1) * 256]
                if j == 1:
                    o_ref[:, t * 256:(t + 1) * 256] = yn.astype(o_ref.dtype)
                    continue
                for hh in range(256 // LANES):
                    hd = t * (256 // LANES) + hh
                    qh = yn[:, hh * LANES:(hh + 1) * LANES]
                    qzt_ref[hd, 0, :, :tm] = jnp.where(lane < dh, qh, 0.0).T.astype(qzt_ref.dtype)
                    qzt_ref[hd, 0, :, tm:] = jnp.where(lane >= dh, qh, 0.0).T.astype(qzt_ref.dtype)


def _in_projection(x, g, sc, sh, w_main, w_gate, qk_gain, bd):
    s = x.shape[0]
    tm = min(ATT_BLOCK, s)
    nt = s // tm
    const = lambda i: (0, 0)
    vec = pl.BlockSpec((1, D_MODEL), const)
    resident = pl.Buffered(1)
    return pl.pallas_call(
        _proj_kernel,
        grid=(nt,),
        in_specs=[
            pl.BlockSpec((tm, D_MODEL), lambda i: (i, 0)),
            vec, vec, vec,
            pl.BlockSpec((D_MODEL, w_main.shape[1]), const, pipeline_mode=resident),
            pl.BlockSpec((D_MODEL, GATE_PAD), const, pipeline_mode=resident),
            pl.BlockSpec((2, 1, D_MODEL), lambda i: (0, 0, 0)),
            pl.BlockSpec((256, 256), const),
        ],
        out_specs=[
            pl.BlockSpec((tm, N_ROWMAJOR), lambda i: (i, 0)),
            pl.BlockSpec((tm, GATE_PAD), lambda i: (i, 0)),
            pl.BlockSpec((A_HEADS, 1, LANES, 2 * tm), lambda i: (0, i, 0, 0)),
            pl.BlockSpec((A_HEADS, 1, LANES, tm), lambda i: (0, i, 0, 0)),
        ],
        out_shape=[
            jax.ShapeDtypeStruct((s, N_ROWMAJOR), jnp.bfloat16),
            jax.ShapeDtypeStruct((s, GATE_PAD), jnp.float32),
            jax.ShapeDtypeStruct((A_HEADS, nt, LANES, 2 * tm), jnp.bfloat16),
            jax.ShapeDtypeStruct((A_HEADS, nt, LANES, tm), jnp.bfloat16),
        ],
        compiler_params=pltpu.CompilerParams(
            dimension_semantics=("arbitrary",), vmem_limit_bytes=VMEM_LIMIT),
        name="in_projection",
    )(x, g, sc, sh, w_main, w_gate, qk_gain, bd)


def _attn_kernel(lam_init, qzt_ref, k_ref, vt_ref, ones_ref, bias_ref, lamqk_ref, g_ref, o_ref,
                 sa_scr, sb_scr, mxa_scr, mxb_scr, m_scr, acc_scr):
    i = pl.program_id(1)
    bq = qzt_ref.shape[1] // 2
    bk = bq
    dv = 2 * A_HEAD_DIM

    m_scr[...] = jnp.full(m_scr.shape, NEG_BIG, jnp.float32)
    acc_scr[...] = jnp.zeros(acc_scr.shape, jnp.float32)

    def logits(j, s_ref, mx_ref):
        kb = k_ref[pl.ds(pl.multiple_of(j * bk, bk), bk), :]
        st = _f32dot(kb, qzt_ref[...])
        s_ref[...] = st
        mx_ref[...] = jnp.max(st, axis=0, keepdims=True)

    def consume(j, s_ref, mx_ref, bias_idx):
        st = s_ref[...]
        if bias_idx is None:
            mx = mx_ref[...]
        else:
            tile = bias_ref[bias_idx]
            st = st + jnp.concatenate([tile, tile], axis=1)
            mx = jnp.max(st, axis=0, keepdims=True)
        m_old = m_scr[...]
        m_new = jnp.maximum(m_old, mx)
        pt = jnp.exp2(st - m_new).astype(jnp.bfloat16)
        vt = jnp.concatenate([vt_ref[j], ones_ref[...]], axis=0)
        pv = _f32dot(vt, pt)
        acc_scr[...] = acc_scr[...] * jnp.exp2(m_old - m_new) + pv
        m_scr[...] = m_new

    NEAR, DIAG = 1, 0
    n_far = jnp.maximum(i - 1, 0)
    buf_a = (sa_scr, mxa_scr)
    buf_b = (sb_scr, mxb_scr)
    logits(0, *buf_a)

    def far_pair(j):
        logits(j + 1, *buf_b)
        consume(j, *buf_a, None)
        logits(j + 2, *buf_a)
        consume(j + 1, *buf_b, None)

    def far_sixteen(t, carry):
        for pair in range(8):
            far_pair(16 * t + 2 * pair)
        return carry

    lax.fori_loop(0, n_far // 16, far_sixteen, 0)

    @pl.when(n_far % 16 >= 8)
    def _():
        for pair in range(4):
            far_pair(16 * (n_far // 16) + 2 * pair)

    @pl.when(n_far % 8 >= 4)
    def _():
        far_pair(8 * (n_far // 8))
        far_pair(8 * (n_far // 8) + 2)

    @pl.when(n_far % 4 >= 2)
    def _():
        far_pair(4 * (n_far // 4))

    @pl.when(i == 0)
    def _():
        consume(i, *buf_a, DIAG)

    @pl.when(jnp.logical_and(i > 0, n_far % 2 == 0))
    def _():
        logits(i, *buf_b)
        consume(i - 1, *buf_a, NEAR)
        consume(i, *buf_b, DIAG)

    @pl.when(n_far % 2 == 1)
    def _():
        logits(i - 1, *buf_b)
        consume(i - 2, *buf_a, None)
        logits(i, *buf_a)
        consume(i - 1, *buf_b, NEAR)
        consume(i, *buf_a, DIAG)

    lq = lamqk_ref[...]
    e1 = jnp.exp(jnp.sum(lq[0:1] * lq[1:2], axis=1, keepdims=True))
    e2 = jnp.exp(jnp.sum(lq[2:3] * lq[3:4], axis=1, keepdims=True))
    lam = e1 - e2 + lam_init

    o = acc_scr[:dv, :] * (1.0 / acc_scr[dv:dv + 1, :])
    ot = o[:, :bq] - lam * o[:, bq:]
    ms = jnp.mean(ot * ot, axis=0, keepdims=True)
    on = ot * lax.rsqrt(ms + EPS) * g_ref[...] * (1.0 - lam_init)
    o_ref[...] = on.T.astype(o_ref.dtype)


def _diff_attention(qzt, proj, vt, bias_tiles, lam_qk, subln_col, lam_init):
    s = proj.shape[0]
    bq = min(ATT_BLOCK, s)
    nq = s // bq
    pad_rows = 16
    ones_rows = jnp.zeros((pad_rows, bq), jnp.bfloat16).at[0].set(1.0)
    return pl.pallas_call(
        functools.partial(_attn_kernel, lam_init),
        grid=(A_HEADS, nq),
        in_specs=[
            pl.BlockSpec((None, None, LANES, 2 * bq), lambda h, i: (h, i, 0, 0)),
            pl.BlockSpec((s, LANES), lambda h, i: (0, h)),
            pl.BlockSpec((None, nq, LANES, bq), lambda h, i: (h, 0, 0, 0)),
            pl.BlockSpec((pad_rows, bq), lambda h, i: (0, 0)),
            pl.BlockSpec((None, 2, bq, bq), lambda h, i: (h, 0, 0, 0)),
            pl.BlockSpec((4, A_HEAD_DIM), lambda h, i: (0, 0)),
            pl.BlockSpec((LANES, 1), lambda h, i: (0, 0)),
        ],
        out_specs=pl.BlockSpec((bq, LANES), lambda h, i: (i, h)),
        out_shape=jax.ShapeDtypeStruct((s, A_WIDTH), jnp.bfloat16),
        scratch_shapes=[
            pltpu.VMEM((bq, 2 * bq), jnp.float32),
            pltpu.VMEM((bq, 2 * bq), jnp.float32),
            pltpu.VMEM((1, 2 * bq), jnp.float32),
            pltpu.VMEM((1, 2 * bq), jnp.float32),
            pltpu.VMEM((1, 2 * bq), jnp.float32),
            pltpu.VMEM((LANES + pad_rows, 2 * bq), jnp.float32),
        ],
        compiler_params=pltpu.CompilerParams(
            dimension_semantics=("arbitrary", "arbitrary"), vmem_limit_bytes=VMEM_LIMIT),
        name="diff_attention",
    )(qzt, proj, vt, ones_rows, bias_tiles, lam_qk, subln_col)


def _t5_bucket(dist):
    max_exact = NUM_BUCKETS // 2
    nf = jnp.maximum(dist, 1).astype(jnp.float32)
    large = max_exact + (jnp.log(nf / max_exact) / math.log(MAX_DISTANCE / max_exact)
                         * (NUM_BUCKETS - max_exact)).astype(jnp.int32)
    large = jnp.minimum(large, NUM_BUCKETS - 1)
    return jnp.where(dist < max_exact, dist, large)


def _attention_bias_tiles(rel_table, s):
    bq = min(ATT_BLOCK, s)
    dists = jnp.arange(2 * bq)
    bias = rel_table[_t5_bucket(dists)].T.astype(jnp.float32)
    far = rel_table[_t5_bucket(jnp.array([s - 1]))].T.astype(jnp.float32)
    bias = (bias - far) * LOG2E
    period = 2 * bq + 1
    masked = jnp.full((A_HEADS, bq + 1), NEG_BIG, jnp.float32)
    u_diag = jnp.concatenate([bias[:, :bq], masked], axis=1)
    u_near = jnp.concatenate([bias[:, bq:], jnp.zeros((A_HEADS, 1), jnp.float32), bias[:, :bq]], axis=1)
    u = jnp.stack([u_diag, u_near], axis=1)
    flat = jnp.tile(u, (1, 1, bq))[:, :, :bq * 2 * bq]
    return flat.reshape(A_HEADS, 2, bq, 2 * bq)[..., :bq]


def _mlstm_head(hd, c, q_ref, k_ref, v_ref, o_ref, gate_ref, gbias_ref, cwq_ref, cwk_ref, cbq_ref, cbk_ref,
                shift_ref, hg_ref, y_ref, state_scr, m_scr, pq_scr, pk_scr):
    L = q_ref.shape[0]
    dh = M_HEAD_DIM

    @pl.when(c == 0)
    def _():
        for scr in (state_scr, m_scr, pq_scr, pk_scr):
            scr[...] = jnp.zeros(scr.shape, jnp.float32)

    row8 = lax.broadcasted_iota(jnp.int32, (8, dh), 0)

    def conv_silu(u_ref, tail_scr, w_ref, b_ref):
        ub = u_ref[...]
        u = ub.astype(jnp.float32)
        w = w_ref[...]
        taps = _f32dot(shift_ref[...], ub)
        tail = tail_scr[...]
        y = b_ref[...] + w[CONV_WIDTH - 1:CONV_WIDTH] * u
        top = jnp.zeros((8, dh), jnp.float32)
        for back in range(1, CONV_WIDTH):
            wb = w[CONV_WIDTH - 1 - back:CONV_WIDTH - back]
            y = y + wb * taps[(back - 1) * L:back * L]
            top = top + wb * jnp.where(row8 < back, pltpu.roll(tail, back, 0), 0.0)
        y = jnp.concatenate([y[:8] + top, y[8:]], axis=0)
        tail_scr[...] = u[L - 8:]
        half = 0.5 * y
        return half + half * jnp.tanh(half)

    qc = conv_silu(q_ref, pq_scr, cwq_ref, cbq_ref)
    kc = conv_silu(k_ref, pk_scr, cwk_ref, cbk_ref) * (dh ** -0.5)
    qb = qc.astype(jnp.bfloat16)
    kb = kc.astype(jnp.bfloat16)

    gb = gbias_ref[...]
    gates = gate_ref[...] + gb
    sub = lax.broadcasted_iota(jnp.int32, gates.shape, 0)
    ig_row = jnp.sum(jnp.where(sub == hd, gates, 0.0), axis=0, keepdims=True)
    fg_row = jnp.sum(jnp.where(sub == hd + M_HEADS, gates, 0.0), axis=0, keepdims=True)
    lf_row = jax.nn.log_sigmoid(fg_row)

    t_idx = lax.broadcasted_iota(jnp.int32, (L, L), 0)
    s_idx = lax.broadcasted_iota(jnp.int32, (L, L), 1)
    tril = s_idx <= t_idx
    eye = s_idx == t_idx

    def to_col(r):
        return jnp.sum(jnp.where(eye, r, 0.0), axis=1, keepdims=True)

    def to_row(col):
        return jnp.sum(jnp.where(eye, col, 0.0), axis=0, keepdims=True)

    bcum_col = jnp.sum(jnp.where(tril, lf_row, 0.0), axis=1, keepdims=True)
    bcum_row = to_row(bcum_col)
    ig_col = to_col(ig_row)
    b_last = bcum_col[L - 1:L, :]
    m_prev = m_scr[0:1, 0:1]

    dmat = jnp.where(tril, bcum_col - bcum_row + ig_row, NEG_BIG)
    inter = bcum_col + m_prev
    m_t = jnp.maximum(jnp.max(dmat, axis=1, keepdims=True), inter)
    qk = lax.dot_general(qb, kb, (((1,), (1,)), ((), ())), preferred_element_type=jnp.float32)
    scores = qk * jnp.exp(dmat - m_t)
    a_inter = jnp.exp(inter - m_t)

    lane_w = lax.broadcasted_iota(jnp.int32, (L, LANES), 1)
    v_ext = jnp.concatenate(
        [v_ref[...], jnp.where(lane_w == 0, 1.0, 0.0).astype(jnp.bfloat16)], axis=1)
    state = state_scr[...]
    res = _f32dot(scores.astype(jnp.bfloat16), v_ext) + a_inter * _f32dot(qb, state.astype(jnp.bfloat16))
    num = res[:, :dh]
    den = res[:, dh:dh + 1]
    h = num / jnp.maximum(jnp.abs(den), jnp.exp(-m_t))

    g_col = b_last - bcum_col + ig_col
    m_new = jnp.maximum(b_last + m_prev, jnp.max(g_col, axis=0, keepdims=True))
    wk = jnp.exp(g_col - m_new)
    decay = jnp.exp(b_last + m_prev - m_new)
    kw = (kc * wk).astype(jnp.bfloat16)
    upd = lax.dot_general(kw, v_ext, (((0,), (0,)), ((), ())), preferred_element_type=jnp.float32)
    state_scr[...] = decay * state + upd
    m_scr[...] = jnp.broadcast_to(m_new, m_scr.shape)

    hn = h * lax.rsqrt(jnp.mean(h * h, axis=-1, keepdims=True) + EPS) * hg_ref[...]
    o_gate = 0.5 + 0.5 * jnp.tanh(0.5 * o_ref[...].astype(jnp.float32))
    y_ref[...] = (o_gate * hn).astype(y_ref.dtype)


def _mlstm_kernel(q_ref, k_ref, v_ref, o_ref, gate_ref, gbias_ref, cwq_ref, cwk_ref, cbq_ref, cbk_ref,
                  shift_ref, hg_ref, y_ref, state_scr, m_scr, pq_scr, pk_scr):
    dh = M_HEAD_DIM
    for hh in range(M_HEADS_PER_STEP):
        cols = pl.ds(hh * dh, dh)
        _mlstm_head(pl.program_id(0) * M_HEADS_PER_STEP + hh, pl.program_id(1),
                    q_ref.at[:, cols], k_ref.at[:, cols], v_ref.at[:, cols], o_ref.at[:, cols],
                    gate_ref, gbias_ref, cwq_ref.at[:, cols], cwk_ref.at[:, cols],
                    cbq_ref.at[:, cols], cbk_ref.at[:, cols], shift_ref, hg_ref, y_ref.at[:, cols],
                    state_scr.at[hh], m_scr.at[hh], pq_scr.at[hh], pk_scr.at[hh])


def _mlstm(proj, gates_t, gate_bias, conv_w, conv_b, mhn_g):
    s = proj.shape[0]
    L = min(M_CHUNK, s)
    nc = s // L
    dh = M_HEAD_DIM
    hps = M_HEADS_PER_STEP
    w = hps * dh
    ng = M_HEADS // hps
    base = D_MODEL // w
    taps = jnp.arange((CONV_WIDTH - 1) * L)
    shift = (jnp.arange(L)[None, :] == (taps % L - taps // L - 1)[:, None]).astype(jnp.bfloat16)
    return pl.pallas_call(
        _mlstm_kernel,
        grid=(ng, nc),
        in_specs=[
            pl.BlockSpec((L, w), lambda g, c: (c, base + g)),
            pl.BlockSpec((L, w), lambda g, c: (c, base + ng + g)),
            pl.BlockSpec((L, w), lambda g, c: (c, base + 2 * ng + g)),
            pl.BlockSpec((L, w), lambda g, c: (c, base + 3 * ng + g)),
            pl.BlockSpec((2 * M_HEADS, L), lambda g, c: (0, c)),
            pl.BlockSpec((2 * M_HEADS, 1), lambda g, c: (0, 0)),
            pl.BlockSpec((CONV_WIDTH, w), lambda g, c: (0, g)),
            pl.BlockSpec((CONV_WIDTH, w), lambda g, c: (0, ng + g)),
            pl.BlockSpec((1, w), lambda g, c: (0, g)),
            pl.BlockSpec((1, w), lambda g, c: (0, ng + g)),
            pl.BlockSpec(((CONV_WIDTH - 1) * L, L), lambda g, c: (0, 0)),
            pl.BlockSpec((1, dh), lambda g, c: (0, 0)),
        ],
        out_specs=pl.BlockSpec((L, w), lambda g, c: (c, g)),
        out_shape=jax.ShapeDtypeStruct((s, D_MODEL), jnp.bfloat16),
        scratch_shapes=[
            pltpu.VMEM((hps, dh, STATE_W), jnp.float32),
            pltpu.VMEM((hps, 8, LANES), jnp.float32),
            pltpu.VMEM((hps, 8, dh), jnp.float32),
            pltpu.VMEM((hps, 8, dh), jnp.float32),
        ],
        compiler_params=pltpu.CompilerParams(
            dimension_semantics=("arbitrary", "arbitrary"), vmem_limit_bytes=VMEM_LIMIT),
        name="mlstm",
    )(proj, proj, proj, proj, gates_t, gate_bias, conv_w, conv_w, conv_b, conv_b, shift, mhn_g)


def _merge_kernel(x_ref, ya_ref, ym_ref, ga_ref, gm_ref, wa_ref, wm_ref, wo_ref, gt_ref, o_ref):
    a = _f32dot(ya_ref[...], wa_ref[...])
    b = _f32dot(ym_ref[...], wm_ref[...])
    merged = (jax.nn.sigmoid(ga_ref[...].astype(jnp.float32)) * a
              + jax.nn.sigmoid(gm_ref[...].astype(jnp.float32)) * b)
    o_ref[...] = x_ref[...] + gt_ref[...] * _f32dot(merged.astype(jnp.bfloat16), wo_ref[...])


def _merge(x, ya, ym, proj, w_a, w_m, w_out, gt):
    s = x.shape[0]
    tm = min(MLP_TILE, s)
    row = lambda i: (i, 0)
    const = lambda i: (0, 0)
    wspec = pl.BlockSpec((D_MODEL, D_MODEL), const, pipeline_mode=pl.Buffered(1))
    return pl.pallas_call(
        _merge_kernel,
        grid=(s // tm,),
        in_specs=[
            pl.BlockSpec((tm, D_MODEL), row),
            pl.BlockSpec((tm, D_MODEL), row),
            pl.BlockSpec((tm, D_MODEL), row),
            pl.BlockSpec((tm, D_MODEL), lambda i: (i, 5)),
            pl.BlockSpec((tm, D_MODEL), lambda i: (i, 6)),
            wspec, wspec, wspec,
            pl.BlockSpec((1, D_MODEL), const),
        ],
        out_specs=pl.BlockSpec((tm, D_MODEL), row),
        out_shape=jax.ShapeDtypeStruct((s, D_MODEL), jnp.float32),
        compiler_params=pltpu.CompilerParams(
            dimension_semantics=("arbitrary",), vmem_limit_bytes=VMEM_LIMIT),
        name="merge_out_projection",
    )(x, ya, ym, proj, proj, w_a, w_m, w_out, gt)


def _mlp_kernel(x_ref, g_ref, sc_ref, sh_ref, gt_ref, w1_ref, w2_ref, o_ref):
    x = x_ref[...]
    hb = _modulated_norm(x, g_ref[...], sc_ref[...], sh_ref[...]).astype(jnp.bfloat16)
    acc = None
    for t in range(D_FF // D_MODEL):
        u = jnp.maximum(_f32dot(hb, w1_ref[:, t * D_MODEL:(t + 1) * D_MODEL]), 0.0)
        part = _f32dot((u * u).astype(jnp.bfloat16), w2_ref[t * D_MODEL:(t + 1) * D_MODEL, :])
        acc = part if acc is None else acc + part
    o_ref[...] = x + gt_ref[...] * acc


def _mlp(x, g, sc, sh, gt, w1, w2):
    s = x.shape[0]
    tm = min(MLP_TILE, s)
    row = lambda i: (i, 0)
    const = lambda i: (0, 0)
    vec = pl.BlockSpec((1, D_MODEL), const)
    return pl.pallas_call(
        _mlp_kernel,
        grid=(s // tm,),
        in_specs=[
            pl.BlockSpec((tm, D_MODEL), row),
            vec, vec, vec, vec,
            pl.BlockSpec((D_MODEL, D_FF), const, pipeline_mode=pl.Buffered(1)),
            pl.BlockSpec((D_FF, D_MODEL), const, pipeline_mode=pl.Buffered(1)),
        ],
        out_specs=pl.BlockSpec((tm, D_MODEL), row),
        out_shape=jax.ShapeDtypeStruct((s, D_MODEL), jnp.float32),
        compiler_params=pltpu.CompilerParams(
            dimension_semantics=("arbitrary",), vmem_limit_bytes=VMEM_LIMIT),
        name="relu2_mlp",
    )(x, g, sc, sh, gt, w1, w2)


def kernel(x, c, w_ada, b_ada, norm_mix_g, norm_ffn_g, w_in, b_igate, b_fgate, qn_g, kn_g, lam_qk,
           subln_g, rel_table, conv_w, conv_b, mhn_g, w_a, w_m, w_out, w_ff1, w_ff2):
    b, s, d = x.shape
    assert b == 1 and d == D_MODEL
    depth = w_ada.shape[0]
    bf = jnp.bfloat16
    xs = x[0]

    mod = _ada_modulation(c, w_ada, b_ada)
    bias_tiles = _attention_bias_tiles(rel_table, s)
    blk = A_HEAD_DIM
    bd = (jnp.arange(256)[:, None] // blk == jnp.arange(256)[None, :] // blk).astype(bf)

    gate_lo = 7 * D_MODEL
    gate_hi = gate_lo + 2 * M_HEADS
    for l in range(depth):
        lam_init = 0.8 - 0.6 * math.exp(-0.3 * l)
        sh_a, sc_a, gt_a, sh_f, sc_f, gt_f = [mod[l:l + 1, t * d:(t + 1) * d] for t in range(6)]

        w = w_in[l]
        w_main = jnp.concatenate([w[:, :gate_lo], w[:, gate_hi:]], axis=1).astype(bf)
        w_gate = jnp.pad(w[:, gate_lo:gate_hi], ((0, 0), (0, GATE_PAD - 2 * M_HEADS))).astype(bf)
        reps = A_WIDTH // A_HEAD_DIM
        qk_gain = jnp.stack([jnp.tile(qn_g[l], reps) * (A_HEAD_DIM ** -0.5 * LOG2E),
                             jnp.tile(kn_g[l], reps)])[:, None, :]

        proj, gates, qzt, vt = _in_projection(xs, norm_mix_g[l][None], sc_a, sh_a, w_main, w_gate,
                                              qk_gain, bd)

        ya = _diff_attention(qzt, proj, vt, bias_tiles, lam_qk[l], subln_g[l][:, None], lam_init)

        gates_t = gates[:, :2 * M_HEADS].T
        gate_bias = jnp.concatenate([b_igate[l], b_fgate[l]])[:, None]
        ym = _mlstm(proj, gates_t, gate_bias, conv_w[l], conv_b[l][None], mhn_g[l][None])

        xs = _merge(xs, ya, ym, proj, w_a[l].astype(bf), w_m[l].astype(bf), w_out[l].astype(bf), gt_a)
        xs = _mlp(xs, norm_ffn_g[l][None], sc_f, sh_f, gt_f, w_ff1[l].astype(bf), w_ff2[l].astype(bf))
    return xs[None]
```
